```python
import math
import jax
import jax.numpy as jnp
from jax import lax
import numpy as np

D_MODEL = 4096
BATCH = 1
SEQ = 16384
DEPTH = 2
DEC_BATCH = 32
DEC_SEQ = 16
PAST_LEN = 1024

CHUNK = 64
MIX_WIDTH = D_MODEL
HEAD_DIM = 128
ATTN_WIDTH = MIX_WIDTH // 2
N_HEADS = ATTN_WIDTH // HEAD_DIM
N_KV_HEADS = N_HEADS // 4
KV_WIDTH = N_KV_HEADS * HEAD_DIM
N_IDX_HEADS = 32
IDX_HEAD_DIM = 64
IDX_SCALE = (N_IDX_HEADS * IDX_HEAD_DIM) ** -0.5
IDX_TOPK_MAX = 256
QUERY_BLOCK = 128
DN_WIDTH = MIX_WIDTH - ATTN_WIDTH
DN_HEAD_DIM = 128
DN_HEADS = DN_WIDTH // DN_HEAD_DIM
DN_CONV = 4
FFN_CONV = 3
D_FF = 256 * ((8 * D_MODEL // 3 + 255) // 256)
ROPE_THETA = 10000.0
EPS = 1e-6
IN_SPLITS = (ATTN_WIDTH, KV_WIDTH, KV_WIDTH, N_IDX_HEADS * IDX_HEAD_DIM, IDX_HEAD_DIM, N_IDX_HEADS,
             DN_WIDTH, DN_WIDTH, DN_WIDTH, DN_HEADS, DN_HEADS, DN_WIDTH)
IN_COLS = sum(IN_SPLITS)

kernel_name = "hymba_dsa_gdn_convffn_stream_step"


def rms_norm(x, g):
    xf = x.astype(jnp.float32)
    y = xf * lax.rsqrt(jnp.mean(xf * xf, axis=-1, keepdims=True) + EPS)
    return (y * g.astype(jnp.float32)).astype(x.dtype)


def l2_normalize(x):
    xf = x.astype(jnp.float32)
    return xf * lax.rsqrt(jnp.sum(xf * xf, axis=-1, keepdims=True) + EPS)


def rotary(x, pos):
    half = x.shape[-1] // 2
    inv_freq = jnp.float32(ROPE_THETA) ** (-jnp.arange(half, dtype=jnp.float32) / half)
    ang = pos.astype(jnp.float32)[:, None] * inv_freq[None, :]
    cos = jnp.cos(ang)[None, :, None, :]
    sin = jnp.sin(ang)[None, :, None, :]
    xf = x.astype(jnp.float32)
    x1, x2 = xf[..., :half], xf[..., half:]
    return jnp.concatenate([x1 * cos - x2 * sin, x2 * cos + x1 * sin], axis=-1).astype(x.dtype)


def causal_dwconv(x, buf, w):
    width = w.shape[0]
    t = x.shape[1]
    ext = jnp.concatenate([buf.astype(x.dtype), x], axis=1)
    y = ext[:, 0:t] * w[0]
    for i in range(1, width):
        y = y + ext[:, i:i + t] * w[i]
    return y, ext[:, t:]


def dsa_attend(q, iq, iw, qpos, k, v, ik, kpos, topk):
    b, tq = q.shape[:2]
    raw = jnp.einsum('bthd,bsd->bths', iq, ik)
    score = jnp.einsum('bth,bths->bts', iw.astype(jnp.float32), jax.nn.relu(raw).astype(jnp.float32))
    admissible = (kpos[None, :] // CHUNK) <= (qpos[:, None] // CHUNK)
    score = jnp.where(admissible[None], score, -jnp.inf)
    top_vals, top_idx = lax.top_k(score, topk)
    valid = jnp.isfinite(top_vals)
    gather = jax.vmap(lambda a, i: a[i])
    k_sel = gather(k, top_idx)
    v_sel = gather(v, top_idx)
    qg = q.reshape(b, tq, N_KV_HEADS, N_HEADS // N_KV_HEADS, HEAD_DIM)
    logits = jnp.einsum('btngd,btknd->btngk', qg, k_sel).astype(jnp.float32) * (HEAD_DIM ** -0.5)
    logits = jnp.where(valid[:, :, None, None, :], logits, -jnp.inf)
    p = jax.nn.softmax(logits, axis=-1).astype(v.dtype)
    o = jnp.einsum('btngk,btknd->btngd', p, v_sel)
    return o.reshape(b, tq, N_HEADS * HEAD_DIM)


def gated_delta_rule(q, k, v, g, beta, state, chunk):
    b, t, h, dk = q.shape
    dv = v.shape[-1]
    n = t // chunk
    f32 = jnp.float32

    def to_chunks(a):
        a = a.astype(f32).reshape(b, n, chunk, h, *a.shape[3:])
        return jnp.moveaxis(a, 3, 1)

    q = to_chunks(q) * (dk ** -0.5)
    k = to_chunks(k)
    v = to_chunks(v)
    beta = to_chunks(beta)
    g = jnp.cumsum(to_chunks(g), axis=-1)
    pos = jnp.arange(chunk)
    causal = pos[:, None] >= pos[None, :]
    strict = pos[:, None] > pos[None, :]
    decay = jnp.exp(jnp.where(causal, g[..., :, None] - g[..., None, :], -jnp.inf))
    kk = jnp.einsum('bhncd,bhnjd->bhncj', k, k)
    m = jnp.where(strict, beta[..., None] * kk * decay, 0.0)
    a = jnp.eye(chunk, dtype=f32) + m
    rhs = jnp.concatenate([v * beta[..., None], k * (beta * jnp.exp(g))[..., None]], axis=-1)
    sol = lax.linalg.triangular_solve(a, rhs, left_side=True, lower=True)
    u, w = sol[..., :dv], sol[..., dv:]
    qk = jnp.einsum('bhncd,bhnjd->bhncj', q, k) * decay
    g_last = g[..., -1]

    def step(s, xs):
        q_c, k_c, u_c, w_c, g_c, gl, qk_c = xs
        v_new = u_c - jnp.einsum('bhcd,bhdv->bhcv', w_c, s)
        o = (jnp.einsum('bhcd,bhdv->bhcv', q_c * jnp.exp(g_c)[..., None], s)
             + jnp.einsum('bhcj,bhjv->bhcv', qk_c, v_new))
        s = (s * jnp.exp(gl)[..., None, None]
             + jnp.einsum('bhcd,bhcv->bhdv', k_c * jnp.exp(gl[..., None] - g_c)[..., None], v_new))
        return s, o

    xs = tuple(jnp.moveaxis(z, 2, 0) for z in (q, k, u, w, g, g_last, qk))
    s_final, o = lax.scan(step, state.astype(f32), xs)
    o = jnp.moveaxis(jnp.moveaxis(o, 0, 2), 1, 3).reshape(b, t, h, dv)
    return o, s_final


def encoder_layer(x, past_k, past_v, past_ik, dn_buf, dn_state, ffn_buf,
                  norm1_g, w_in, dn_conv_w, dn_a_log, dn_dt_bias, dn_norm_g, w_out,
                  norm2_g, ffn_w_up, ffn_conv_w, ffn_w_down):
    b, t, _ = x.shape
    past_len = past_k.shape[1]
    pos = past_len + jnp.arange(t, dtype=jnp.int32)
    h = rms_norm(x, norm1_g)
    proj = h @ w_in
    points = np.cumsum(IN_SPLITS)[:-1].tolist()
    a_q, a_k, a_v, i_q, i_k, i_w, d_q, d_k, d_v, d_b, d_a, d_z = jnp.split(proj, points, axis=-1)

    a_q = rotary(a_q.reshape(b, t, N_HEADS, HEAD_DIM), pos)
    a_k = rotary(a_k.reshape(b, t, N_KV_HEADS, HEAD_DIM), pos)
    a_v = a_v.reshape(b, t, N_KV_HEADS, HEAD_DIM)
    i_q = rotary(i_q.reshape(b, t, N_IDX_HEADS, IDX_HEAD_DIM), pos)
    i_k = rotary(i_k.reshape(b, t, 1, IDX_HEAD_DIM), pos)[:, :, 0]
    i_w = i_w * IDX_SCALE
    k_all = jnp.concatenate([past_k.astype(x.dtype), a_k], axis=1)
    v_all = jnp.concatenate([past_v.astype(x.dtype), a_v], axis=1)
    ik_all = jnp.concatenate([past_ik.astype(x.dtype), i_k], axis=1)
    n_keys = k_all.shape[1]
    kpos = jnp.arange(n_keys, dtype=jnp.int32)
    topk = min(IDX_TOPK_MAX, n_keys // 4)
    if t > QUERY_BLOCK:
        nb = t // QUERY_BLOCK

        def split_blocks(z):
            return jnp.moveaxis(z.reshape(b, nb, QUERY_BLOCK, *z.shape[2:]), 1, 0)

        def block_fn(args):
            qb, iqb, iwb, pb = args
            return dsa_attend(qb, iqb, iwb, pb, k_all, v_all, ik_all, kpos, topk)

        attn = lax.map(block_fn, (split_blocks(a_q), split_blocks(i_q), split_blocks(i_w),
                                  pos.reshape(nb, QUERY_BLOCK)))
        attn = jnp.moveaxis(attn, 0, 1).reshape(b, t, ATTN_WIDTH)
    else:
        attn = dsa_attend(a_q, i_q, i_w, pos, k_all, v_all, ik_all, kpos, topk)

    qkv = jnp.concatenate([d_q, d_k, d_v], axis=-1)
    qkv, new_dn_buf = causal_dwconv(qkv, dn_buf, dn_conv_w)
    qkv = jax.nn.silu(qkv)
    d_q, d_k, d_v = jnp.split(qkv, 3, axis=-1)
    q_dn = l2_normalize(d_q.reshape(b, t, DN_HEADS, DN_HEAD_DIM))
    k_dn = l2_normalize(d_k.reshape(b, t, DN_HEADS, DN_HEAD_DIM))
    v_dn = d_v.reshape(b, t, DN_HEADS, DN_HEAD_DIM)
    beta = jax.nn.sigmoid(d_b.astype(jnp.float32))
    g = -jnp.exp(dn_a_log.astype(jnp.float32)) * jax.nn.softplus(d_a.astype(jnp.float32) + dn_dt_bias.astype(jnp.float32))
    dn_chunk = CHUNK if t % CHUNK == 0 else t
    o_dn, new_state = gated_delta_rule(q_dn, k_dn, v_dn, g, beta, dn_state, dn_chunk)
    gate = jax.nn.silu(d_z.reshape(b, t, DN_HEADS, DN_HEAD_DIM).astype(jnp.float32))
    o_dn = (rms_norm(o_dn, dn_norm_g) * gate).astype(x.dtype).reshape(b, t, DN_WIDTH)

    x = x + jnp.concatenate([attn, o_dn], axis=-1) @ w_out

    h = rms_norm(x, norm2_g)
    up, new_ffn_buf = causal_dwconv(h @ ffn_w_up, ffn_buf, ffn_conv_w)
    u_gate, u_val = jnp.split(up, 2, axis=-1)
    x = x + (jax.nn.silu(u_gate) * u_val) @ ffn_w_down
    return x, a_k, a_v, i_k, new_dn_buf, new_state.astype(dn_state.dtype), new_ffn_buf


def setup_inputs(seed: int = 0) -> dict:
    key = jax.random.key(seed)
    ks = jax.random.split(key, 20)
    f32 = jnp.float32

    def nrm(k, shape, scale):
        return jax.random.normal(k, shape, f32) * scale

    dt = jnp.exp(jax.random.uniform(ks[12], (DEPTH, DN_HEADS), f32, math.log(1e-3), math.log(1e-1)))
    return {
        "x_prompt": nrm(ks[0], (BATCH, SEQ, D_MODEL), 1.0),
        "x_sample": nrm(ks[1], (DEC_BATCH, DEC_SEQ, D_MODEL), 1.0),
        "cache_k": nrm(ks[2], (DEPTH, DEC_BATCH, PAST_LEN, N_KV_HEADS, HEAD_DIM), 1.0),
        "cache_v": nrm(ks[3], (DEPTH, DEC_BATCH, PAST_LEN, N_KV_HEADS, HEAD_DIM), 1.0),
        "cache_kidx": nrm(ks[4], (DEPTH, DEC_BATCH, PAST_LEN, IDX_HEAD_DIM), 1.0),
        "state_dn_conv": nrm(ks[5], (DEPTH, DEC_BATCH, DN_CONV - 1, 3 * DN_WIDTH), 1.0),
        "state_dn": nrm(ks[6], (DEPTH, DEC_BATCH, DN_HEADS, DN_HEAD_DIM, DN_HEAD_DIM), 0.1),
        "state_ffn_conv": nrm(ks[7], (DEPTH, DEC_BATCH, FFN_CONV - 1, 2 * D_FF), 1.0),
        "norm1_g": 1.0 + nrm(ks[8], (DEPTH, D_MODEL), 0.05),
        "w_in": nrm(ks[9], (DEPTH, D_MODEL, IN_COLS), D_MODEL ** -0.5),
        "dn_conv_w": nrm(ks[10], (DEPTH, DN_CONV, 3 * DN_WIDTH), DN_CONV ** -0.5),
        "dn_a_log": jnp.log(jax.random.uniform(ks[11], (DEPTH, DN_HEADS), f32, 1.0, 16.0)),
        "dn_dt_bias": dt + jnp.log(-jnp.expm1(-dt)),
        "dn_norm_g": 1.0 + nrm(ks[13], (DEPTH, DN_HEAD_DIM), 0.05),
        "w_out": nrm(ks[14], (DEPTH, MIX_WIDTH, D_MODEL), MIX_WIDTH ** -0.5),
        "norm2_g": 1.0 + nrm(ks[15], (DEPTH, D_MODEL), 0.05),
        "ffn_w_up": nrm(ks[16], (DEPTH, D_MODEL, 2 * D_FF), D_MODEL ** -0.5),
        "ffn_conv_w": nrm(ks[17], (DEPTH, FFN_CONV, 2 * D_FF), FFN_CONV ** -0.5),
        "ffn_w_down": nrm(ks[18], (DEPTH, D_FF, D_MODEL), D_FF ** -0.5),
        "final_g": 1.0 + nrm(ks[19], (D_MODEL,), 0.05),
    }


def reference(x_prompt, x_sample, cache_k, cache_v, cache_kidx, state_dn_conv, state_dn, state_ffn_conv,
              norm1_g, w_in, dn_conv_w, dn_a_log, dn_dt_bias, dn_norm_g, w_out, norm2_g,
              ffn_w_up, ffn_conv_w, ffn_w_down, final_g):
    b = x_prompt.shape[0]
    dt = x_prompt.dtype
    empty_k = jnp.zeros((b, 0, N_KV_HEADS, HEAD_DIM), dt)
    empty_ik = jnp.zeros((b, 0, IDX_HEAD_DIM), dt)
    zero_dn_buf = jnp.zeros((b, DN_CONV - 1, 3 * DN_WIDTH), dt)
    zero_dn_state = jnp.zeros((b, DN_HEADS, DN_HEAD_DIM, DN_HEAD_DIM), state_dn.dtype)
    zero_ffn_buf = jnp.zeros((b, FFN_CONV - 1, 2 * D_FF), dt)

    xp, xs = x_prompt, x_sample
    st_p, st_s = [], []
    for l in range(DEPTH):
        w = (norm1_g[l], w_in[l], dn_conv_w[l], dn_a_log[l], dn_dt_bias[l], dn_norm_g[l], w_out[l],
             norm2_g[l], ffn_w_up[l], ffn_conv_w[l], ffn_w_down[l])
        xp, *sp = encoder_layer(xp, empty_k, empty_k, empty_ik, zero_dn_buf, zero_dn_state, zero_ffn_buf, *w)
        xs, *ss = encoder_layer(xs, cache_k[l], cache_v[l], cache_kidx[l], state_dn_conv[l], state_dn[l],
                                state_ffn_conv[l], *w)
        st_p.append(sp)
        st_s.append(ss)

    y_prompt = rms_norm(xp, final_g)
    y_sample = rms_norm(xs, final_g)

    def stacked(states, i):
        return jnp.stack([s[i] for s in states], axis=0)

    new_k_prompt = stacked(st_p, 0)
    new_v_prompt = stacked(st_p, 1)
    new_kidx_prompt = stacked(st_p, 2)
    new_dn_conv_prompt = stacked(st_p, 3)
    new_dn_state_prompt = stacked(st_p, 4)
    new_ffn_conv_prompt = stacked(st_p, 5)
    new_k_sample = stacked(st_s, 0)
    new_v_sample = stacked(st_s, 1)
    new_kidx_sample = stacked(st_s, 2)
    new_dn_conv_sample = stacked(st_s, 3)
    new_dn_state_sample = stacked(st_s, 4)
    new_ffn_conv_sample = stacked(st_s, 5)
    return (y_prompt, y_sample,
            new_k_prompt, new_v_prompt, new_kidx_prompt, new_dn_conv_prompt, new_dn_state_prompt, new_ffn_conv_prompt,
            new_k_sample, new_v_sample, new_kidx_sample, new_dn_conv_sample, new_dn_state_sample, new_ffn_conv_sample)
```

```python
import functools

import numpy as np
import jax
import jax.numpy as jnp
from jax import lax
from jax.experimental import pallas as pl
from jax.experimental.pallas import tpu as pltpu

F32 = jnp.float32
BF16 = jnp.bfloat16
I32 = jnp.int32

CHUNK = 64
HEAD_DIM = 128
N_HEADS = 16
N_KV_HEADS = 4
KV_GROUP = N_HEADS // N_KV_HEADS
ATTN_WIDTH = N_HEADS * HEAD_DIM
KV_WIDTH = N_KV_HEADS * HEAD_DIM
N_IDX_HEADS = 32
IDX_HEAD_DIM = 64
IDX_SCALE = (N_IDX_HEADS * IDX_HEAD_DIM) ** -0.5
IDX_TOPK_MAX = 256
DN_HEADS = 16
DN_HEAD_DIM = 128
DN_WIDTH = DN_HEADS * DN_HEAD_DIM
DN_CONV = 4
FFN_CONV = 3
ROPE_THETA = 10000.0
EPS = 1e-6

LANES = 128
SUBLANES = 8
VMEM_LIMIT = 56 * 1024 * 1024

INT_MIN = -(2 ** 31)
NEG_BIG = -1e30


def _cparams(sem):
    return pltpu.CompilerParams(dimension_semantics=sem, vmem_limit_bytes=VMEM_LIMIT)


def _pick(n, prefs):
    for p in prefs:
        if n % p == 0:
            return p
    raise ValueError(f"no tile in {prefs} divides {n}")


def _norm_matmul_kernel(x_ref, g_ref, w_ref, o_ref, h_ref):
    @pl.when(pl.program_id(1) == 0)
    def _():
        x = x_ref[...]
        ms = jnp.mean(x * x, axis=-1, keepdims=True)
        h_ref[...] = (x * lax.rsqrt(ms + EPS) * g_ref[...]).astype(BF16)

    o_ref[...] = jnp.dot(h_ref[...], w_ref[...], preferred_element_type=F32)


def norm_matmul(x, g, w):
    n, d = x.shape
    cols = w.shape[1]
    tm = _pick(n, (512, 256, 128, 64, 32, 16, 8))
    tn = _pick(cols, (640, 512, 256, 128))
    return pl.pallas_call(
        _norm_matmul_kernel,
        grid=(n // tm, cols // tn),
        in_specs=[pl.BlockSpec((tm, d), lambda i, j: (i, 0)),
                  pl.BlockSpec((1, d), lambda i, j: (0, 0)),
                  pl.BlockSpec((d, tn), lambda i, j: (0, j))],
        out_specs=pl.BlockSpec((tm, tn), lambda i, j: (i, j)),
        out_shape=jax.ShapeDtypeStruct((n, cols), F32),
        scratch_shapes=[pltpu.VMEM((tm, d), BF16)],
        compiler_params=_cparams(("parallel", "arbitrary")),
        name="norm_matmul",
    )(x, g.reshape(1, d), w)


def _matmul_res_kernel(a_ref, w_ref, x_ref, o_ref):
    o_ref[...] = x_ref[...] + jnp.dot(a_ref[...], w_ref[...], preferred_element_type=F32)


def matmul_res(a, w, x):
    n, k = a.shape
    cols = w.shape[1]
    tm = _pick(n, (512, 256, 128, 64, 32, 16, 8))
    tn = _pick(cols, (256,) if k > 8192 else (1024, 512, 256, 128))
    return pl.pallas_call(
        _matmul_res_kernel,
        grid=(n // tm, cols // tn),
        in_specs=[pl.BlockSpec((tm, k), lambda i, j: (i, 0)),
                  pl.BlockSpec((k, tn), lambda i, j: (0, j)),
                  pl.BlockSpec((tm, tn), lambda i, j: (i, j))],
        out_specs=pl.BlockSpec((tm, tn), lambda i, j: (i, j)),
        out_shape=jax.ShapeDtypeStruct((n, cols), F32),
        compiler_params=_cparams(("parallel", "arbitrary")),
        name="matmul_res",
    )(a, w, x)


def _rms_norm_kernel(x_ref, g_ref, o_ref):
    x = x_ref[...]
    ms = jnp.mean(x * x, axis=-1, keepdims=True)
    o_ref[...] = x * lax.rsqrt(ms + EPS) * g_ref[...]


def rms_norm_rows(x, g):
    n, d = x.shape
    tm = _pick(n, (512, 256, 128, 64, 32, 16, 8))
    return pl.pallas_call(
        _rms_norm_kernel,
        grid=(n // tm,),
        in_specs=[pl.BlockSpec((tm, d), lambda i: (i, 0)), pl.BlockSpec((1, d), lambda i: (0, 0))],
        out_specs=pl.BlockSpec((tm, d), lambda i: (i, 0)),
        out_shape=jax.ShapeDtypeStruct((n, d), F32),
        compiler_params=_cparams(("parallel",)),
        name="rms_norm",
    )(x, g.reshape(1, d))


def _ffn_up_kernel(x_ref, g_ref, wg_ref, wv_ref, cg_ref, cv_ref, hg_ref, hv_ref,
                   a_ref, sg_ref, sv_ref,
                   h_ref, eg_ref, ev_ref, cyg_ref, cyv_ref, *, tm, step, halo):
    i = pl.program_id(0)
    j = pl.program_id(1)

    @pl.when(j == 0)
    def _():
        x = x_ref[...]
        ms = jnp.mean(x * x, axis=-1, keepdims=True)
        h_ref[...] = (x * lax.rsqrt(ms + EPS) * g_ref[...]).astype(BF16)

    def conv_half(w_ref, c_ref, hist_ref, e_ref, cy_ref, st_ref):
        up = jnp.dot(h_ref[...], w_ref[...], preferred_element_type=F32)
        e_ref[halo:halo + tm, :] = up

        @pl.when(i == 0)
        def _():
            e_ref[0:halo, :] = hist_ref[...]

        @pl.when(i > 0)
        def _():
            e_ref[0:halo, :] = cy_ref[j]

        c = c_ref[...]
        y = (e_ref[halo - 2 * step:halo - 2 * step + tm, :] * c[0:1, :]
             + e_ref[halo - step:halo - step + tm, :] * c[1:2, :]
             + up * c[2:3, :])
        tail = e_ref[tm:tm + halo, :]
        cy_ref[j] = tail
        st_ref[...] = tail
        return y

    yg = conv_half(wg_ref, cg_ref, hg_ref, eg_ref, cyg_ref, sg_ref)
    yv = conv_half(wv_ref, cv_ref, hv_ref, ev_ref, cyv_ref, sv_ref)
    a_ref[...] = (yg * jax.nn.sigmoid(yg) * yv).astype(BF16)


def ffn_up(x, g, w_up, conv_w, hist, step):
    n, d = x.shape
    f2 = w_up.shape[1]
    f = f2 // 2
    halo = hist.shape[0]
    tm = _pick(n, (512, 256, 128, 64, 32, 16, 8))
    tn = _pick(f, (256, 128))
    nj = f // tn
    kern = functools.partial(_ffn_up_kernel, tm=tm, step=step, halo=halo)
    act, tail_g, tail_v = pl.pallas_call(
        kern,
        grid=(n // tm, nj),
        in_specs=[pl.BlockSpec((tm, d), lambda i, j: (i, 0)),
                  pl.BlockSpec((1, d), lambda i, j: (0, 0)),
                  pl.BlockSpec((d, tn), lambda i, j: (0, j)),
                  pl.BlockSpec((d, tn), lambda i, j, nj=nj: (0, nj + j)),
                  pl.BlockSpec((FFN_CONV, tn), lambda i, j: (0, j)),
                  pl.BlockSpec((FFN_CONV, tn), lambda i, j, nj=nj: (0, nj + j)),
                  pl.BlockSpec((halo, tn), lambda i, j: (0, j)),
                  pl.BlockSpec((halo, tn), lambda i, j, nj=nj: (0, nj + j))],
        out_specs=[pl.BlockSpec((tm, tn), lambda i, j: (i, j)),
                   pl.BlockSpec((None, halo, tn), lambda i, j: (i, 0, j)),
                   pl.BlockSpec((None, halo, tn), lambda i, j: (i, 0, j))],
        out_shape=[jax.ShapeDtypeStruct((n, f), BF16),
                   jax.ShapeDtypeStruct((n // tm, halo, f), F32),
                   jax.ShapeDtypeStruct((n // tm, halo, f), F32)],
        scratch_shapes=[pltpu.VMEM((tm, d), BF16),
                        pltpu.VMEM((tm + halo, tn), F32),
                        pltpu.VMEM((tm + halo, tn), F32),
                        pltpu.VMEM((nj, halo, tn), F32),
                        pltpu.VMEM((nj, halo, tn), F32)],
        compiler_params=_cparams(("arbitrary", "arbitrary")),
        name="ffn_up",
    )(x, g.reshape(1, d), w_up, w_up, conv_w, conv_w, hist, hist)
    return act, jnp.concatenate([tail_g[-1], tail_v[-1]], axis=1)


IDX_HEAD_GROUP = 8


def _dsa_kernel(tb_ref, tqb_ref, tph_ref, tkb_ref, tnkb_ref, tikb_ref, tkvb_ref,
                iq_ref, iw_ref, ik_ref, q_ref, k_ref, v_ref, o_ref,
                keys_ref, sc_ref, thr_ref, wb_ref, qs_ref, m_ref, l_ref, acc_ref,
                *, tq, tk, n_keys, past, topk):
    s = pl.program_id(0)
    ph = tph_ref[s]
    kb = tkb_ref[s]
    nkb = tnkb_ref[s]
    q0 = tqb_ref[s] * tq + past
    ncol = tk // LANES

    @pl.when(ph == 0)
    def _scores():
        @pl.when(kb == 0)
        def _():
            w = iw_ref[0]
            for h in range(N_IDX_HEADS):
                wb_ref[h] = jnp.broadcast_to(w[:, h:h + 1], (tq, LANES))

        ikb = ik_ref[0]
        sc_ref[...] = jnp.zeros((tq, tk), F32)

        def group(gi, carry):
            iqg = iq_ref[0, pl.ds(gi * IDX_HEAD_GROUP, IDX_HEAD_GROUP)]
            raw = lax.dot_general(iqg.reshape(IDX_HEAD_GROUP * tq, IDX_HEAD_DIM), ikb,
                                  (((1,), (1,)), ((), ())), preferred_element_type=F32)
            part = None
            for hh in range(IDX_HEAD_GROUP):
                wv = wb_ref[gi * IDX_HEAD_GROUP + hh]
                wv = jnp.concatenate([wv] * ncol, axis=1) if ncol > 1 else wv
                term = wv * jnp.maximum(raw[hh * tq:(hh + 1) * tq, :], 0.0)
                part = term if part is None else part + term
            sc_ref[...] += part
            return carry

        lax.fori_loop(0, N_IDX_HEADS // IDX_HEAD_GROUP, group, 0)

        sc = sc_ref[...] + 0.0
        bits = pltpu.bitcast(sc, I32)
        key = bits ^ ((bits >> 31) & 0x7FFFFFFF)
        kpos = kb * tk + lax.broadcasted_iota(I32, (tq, tk), 1)
        qpos = q0 + lax.broadcasted_iota(I32, (tq, tk), 0)
        adm = jnp.logical_and((kpos >> 6) <= (qpos >> 6), kpos < n_keys)
        keys_ref[kb] = jnp.where(adm, key, INT_MIN)

    @pl.when(ph == 1)
    def _attend():
        @pl.when(kb == 0)
        def _select():
            def bit_body(it, thr):
                cand = thr + lax.shift_left(jnp.int32(1), 31 - it)

                def blk(b, cnt):
                    hit = (keys_ref[b] >= cand).astype(I32)
                    for c in range(ncol):
                        cnt = cnt + hit[:, c * LANES:(c + 1) * LANES]
                    return cnt

                cnt = lax.fori_loop(0, nkb, blk, jnp.zeros((tq, LANES), I32))
                total = jnp.sum(cnt, axis=1, keepdims=True)
                return jnp.where(total >= topk, cand, thr)

            thr = lax.fori_loop(0, 32, bit_body, jnp.full((tq, 1), INT_MIN, I32))
            thr_ref[...] = jnp.maximum(thr, INT_MIN + 1)

            m_ref[...] = jnp.full(m_ref.shape, NEG_BIG, F32)
            l_ref[...] = jnp.zeros(l_ref.shape, F32)
            acc_ref[...] = jnp.zeros(acc_ref.shape, F32)
            qv = q_ref[0]
            for n in range(N_KV_HEADS):
                for g in range(KV_GROUP):
                    hd = n * KV_GROUP + g
                    qs_ref[n, g * tq:(g + 1) * tq, :] = qv[:, hd * HEAD_DIM:(hd + 1) * HEAD_DIM]

        bias = jnp.where(keys_ref[kb] >= thr_ref[...], 0.0, NEG_BIG).astype(F32)
        bias4 = jnp.concatenate([bias] * KV_GROUP, axis=0)
        kv = k_ref[0]
        vv = v_ref[0]
        for n in range(N_KV_HEADS):
            kn = kv[:, n * HEAD_DIM:(n + 1) * HEAD_DIM]
            vn = vv[:, n * HEAD_DIM:(n + 1) * HEAD_DIM]
            lg = lax.dot_general(qs_ref[n], kn, (((1,), (1,)), ((), ())), preferred_element_type=F32)
            lg = lg * (HEAD_DIM ** -0.5) + bias4
            m_prev = m_ref[n]
            m_new = jnp.maximum(m_prev, jnp.max(lg, axis=1, keepdims=True))
            alpha = jnp.exp(m_prev - m_new)
            p = jnp.exp(lg - m_new)
            l_ref[n] = alpha * l_ref[n] + jnp.sum(p, axis=1, keepdims=True)
            acc_ref[n] = alpha * acc_ref[n] + jnp.dot(p.astype(BF16), vn, preferred_element_type=F32)
            m_ref[n] = m_new

        @pl.when(kb == nkb - 1)
        def _finish():
            for n in range(N_KV_HEADS):
                o = acc_ref[n] / l_ref[n]
                for g in range(KV_GROUP):
                    hd = n * KV_GROUP + g
                    o_ref[0, :, hd * HEAD_DIM:(hd + 1) * HEAD_DIM] = o[g * tq:(g + 1) * tq, :].astype(BF16)


def _dsa_tables(batch, t, tq, tk, n_keys_pad, past, n_keys):
    rows = []
    for b in range(batch):
        for qb in range(t // tq):
            last_q = past + qb * tq + tq - 1
            limit = min(n_keys, (last_q // CHUNK + 1) * CHUNK)
            nkb = -(-limit // tk)
            for ph in range(2):
                for kb in range(nkb):
                    ikb = kb if ph == 0 else nkb - 1
                    kvb = 0 if ph == 0 else kb
                    rows.append((b, qb, ph, kb, nkb, ikb, kvb))
    tab = np.asarray(rows, dtype=np.int32).T
    return [jnp.asarray(tab[r]) for r in range(tab.shape[0])]


def dsa_attention(iq, iw, ik, q, k, v, *, n_keys, past, tq, tk):
    batch, _, t, _ = iq.shape
    lp = ik.shape[1]
    assert t % tq == 0 and lp % tk == 0 and tk % LANES == 0
    topk = min(IDX_TOPK_MAX, n_keys // 4)
    tables = _dsa_tables(batch, t, tq, tk, lp, past, n_keys)
    nsteps = int(tables[0].shape[0])
    nkb_max = lp // tk
    kern = functools.partial(_dsa_kernel, tq=tq, tk=tk, n_keys=n_keys, past=past, topk=topk)
    grid_spec = pltpu.PrefetchScalarGridSpec(
        num_scalar_prefetch=7,
        grid=(nsteps,),
        in_specs=[
            pl.BlockSpec((1, N_IDX_HEADS, tq, IDX_HEAD_DIM), lambda s, tb, tqb, tph, tkb, tnkb, tikb, tkvb: (tb[s], 0, tqb[s], 0)),
            pl.BlockSpec((1, tq, N_IDX_HEADS), lambda s, tb, tqb, tph, tkb, tnkb, tikb, tkvb: (tb[s], tqb[s], 0)),
            pl.BlockSpec((1, tk, IDX_HEAD_DIM), lambda s, tb, tqb, tph, tkb, tnkb, tikb, tkvb: (tb[s], tikb[s], 0)),
            pl.BlockSpec((1, tq, ATTN_WIDTH), lambda s, tb, tqb, tph, tkb, tnkb, tikb, tkvb: (tb[s], tqb[s], 0)),
            pl.BlockSpec((1, tk, KV_WIDTH), lambda s, tb, tqb, tph, tkb, tnkb, tikb, tkvb: (tb[s], tkvb[s], 0)),
            pl.BlockSpec((1, tk, KV_WIDTH), lambda s, tb, tqb, tph, tkb, tnkb, tikb, tkvb: (tb[s], tkvb[s], 0)),
        ],
        out_specs=pl.BlockSpec((1, tq, ATTN_WIDTH), lambda s, tb, tqb, tph, tkb, tnkb, tikb, tkvb: (tb[s], tqb[s], 0)),
        scratch_shapes=[
            pltpu.VMEM((nkb_max, tq, tk), I32),
            pltpu.VMEM((tq, tk), F32),
            pltpu.VMEM((tq, 1), I32),
            pltpu.VMEM((N_IDX_HEADS, tq, LANES), F32),
            pltpu.VMEM((N_KV_HEADS, KV_GROUP * tq, HEAD_DIM), BF16),
            pltpu.VMEM((N_KV_HEADS, KV_GROUP * tq, 1), F32),
            pltpu.VMEM((N_KV_HEADS, KV_GROUP * tq, 1), F32),
            pltpu.VMEM((N_KV_HEADS, KV_GROUP * tq, HEAD_DIM), F32),
        ],
    )
    return pl.pallas_call(
        kern,
        grid_spec=grid_spec,
        out_shape=jax.ShapeDtypeStruct((batch, t, ATTN_WIDTH), BF16),
        compiler_params=_cparams(("arbitrary",)),
        name="dsa_attention",
    )(*tables, iq, iw, ik, q, k, v)


def _mm(a, b):
    return jnp.dot(a.astype(BF16), b.astype(BF16), preferred_element_type=F32)


def _mm_nt(a, b):
    return lax.dot_general(a.astype(BF16), b.astype(BF16), (((1,), (1,)), ((), ())), preferred_element_type=F32)


def _mm_tn(a, b):
    return lax.dot_general(a.astype(BF16), b.astype(BF16), (((0,), (0,)), ((), ())), preferred_element_type=F32)


def _delta_kernel(q_ref, k_ref, v_ref, z_ref, gc_ref, gr_ref, b_ref, gn_ref, s0_ref, o_ref, s_ref,
                  *, chunk, n_double):
    @pl.when(pl.program_id(1) == 0)
    def _():
        s_ref[...] = s0_ref[...]

    row = lax.broadcasted_iota(I32, (chunk, chunk), 0)
    col = lax.broadcasted_iota(I32, (chunk, chunk), 1)
    eye = (row == col).astype(F32)
    gcs = gc_ref[0]
    grs = gr_ref[0, 0]
    betas = b_ref[0]
    gn = gn_ref[...]

    for h in range(DN_HEADS):
        sl = slice(h * DN_HEAD_DIM, (h + 1) * DN_HEAD_DIM)
        qh = q_ref[0, :, sl] * (DN_HEAD_DIM ** -0.5)
        kh = k_ref[0, :, sl]
        vh = v_ref[0, :, sl]
        gc = gcs[:, h:h + 1]
        gr = grs[h:h + 1, :]
        beta = betas[:, h:h + 1]
        decay = jnp.exp(jnp.where(row >= col, gc - gr, -jnp.inf))
        kk = _mm_nt(kh, kh)
        m = jnp.where(row > col, beta * kk * decay, 0.0)
        x = -m
        t = eye + x
        for _ in range(n_double):
            x = _mm(x, x)
            t = t + _mm(t, x)
        eg = jnp.exp(gc)
        u = _mm(t, vh * beta)
        w = _mm(t, kh * (beta * eg))
        qk = _mm_nt(qh, kh) * decay
        s = s_ref[0, h]
        v_new = u - _mm(w, s)
        o = _mm(qh * eg, s) + _mm(qk, v_new)
        gl = gc[chunk - 1:chunk, :]
        s_ref[0, h] = s * jnp.exp(gl) + _mm_tn(kh * jnp.exp(gl - gc), v_new)
        ms = jnp.mean(o * o, axis=-1, keepdims=True)
        on = o * lax.rsqrt(ms + EPS) * gn
        z = z_ref[0, :, sl]
        o_ref[0, :, sl] = (on * (z * jax.nn.sigmoid(z))).astype(BF16)


def delta_rule(q, k, v, z, gc, beta, gnorm, state, chunk):
    batch, t, _ = q.shape
    n = t // chunk
    n_double = max(0, (chunk - 1).bit_length() - 1)
    gr = jnp.swapaxes(gc.reshape(batch, n, chunk, DN_HEADS), 2, 3)
    kern = functools.partial(_delta_kernel, chunk=chunk, n_double=n_double)
    wide = pl.BlockSpec((1, chunk, DN_WIDTH), lambda b, c: (b, c, 0))
    narrow = pl.BlockSpec((1, chunk, DN_HEADS), lambda b, c: (b, c, 0))
    st = pl.BlockSpec((1, DN_HEADS, DN_HEAD_DIM, DN_HEAD_DIM), lambda b, c: (b, 0, 0, 0))
    return pl.pallas_call(
        kern,
        grid=(batch, n),
        in_specs=[wide, wide, wide, wide, narrow,
                  pl.BlockSpec((1, 1, DN_HEADS, chunk), lambda b, c: (b, c, 0, 0)),
                  narrow,
                  pl.BlockSpec((1, DN_HEAD_DIM), lambda b, c: (0, 0)),
                  st],
        out_specs=[wide, st],
        out_shape=[jax.ShapeDtypeStruct((batch, t, DN_WIDTH), BF16),
                   jax.ShapeDtypeStruct(state.shape, F32)],
        compiler_params=_cparams(("parallel", "arbitrary")),
        name="delta_rule",
    )(q, k, v, z, gc, gr, beta, gnorm.reshape(1, DN_HEAD_DIM), state)


IN_SPLITS = (ATTN_WIDTH, KV_WIDTH, KV_WIDTH, N_IDX_HEADS * IDX_HEAD_DIM, IDX_HEAD_DIM, N_IDX_HEADS,
             DN_WIDTH, DN_WIDTH, DN_WIDTH, DN_HEADS, DN_HEADS, DN_WIDTH)


def _rotary(x, pos):
    half = x.shape[-1] // 2
    inv_freq = jnp.float32(ROPE_THETA) ** (-jnp.arange(half, dtype=F32) / half)
    ang = pos.astype(F32)[:, None] * inv_freq[None, :]
    cos = jnp.cos(ang)[None, :, None, :]
    sin = jnp.sin(ang)[None, :, None, :]
    x1, x2 = x[..., :half], x[..., half:]
    return jnp.concatenate([x1 * cos - x2 * sin, x2 * cos + x1 * sin], axis=-1)


def _l2n(x):
    return x * lax.rsqrt(jnp.sum(x * x, axis=-1, keepdims=True) + EPS)


def _layer(x, past_k, past_v, past_ik, dn_buf, dn_state, ffn_buf, lw, *, attn_tiles):
    (norm1_g, w_in, dn_conv_w, dn_a_log, dn_dt_bias, dn_norm_g, w_out, norm2_g, ffn_w_up, ffn_conv_w,
     ffn_w_down) = lw
    b, t, d = x.shape
    past = past_k.shape[1]
    pos = past + jnp.arange(t, dtype=I32)
    x2 = x.reshape(b * t, d)

    proj = norm_matmul(x2, norm1_g, w_in).reshape(b, t, -1)
    points = np.cumsum(IN_SPLITS)[:-1].tolist()
    a_q, a_k, a_v, i_q, i_k, i_w, d_q, d_k, d_v, d_b, d_a, d_z = jnp.split(proj, points, axis=-1)

    a_q = _rotary(a_q.reshape(b, t, N_HEADS, HEAD_DIM), pos)
    a_k = _rotary(a_k.reshape(b, t, N_KV_HEADS, HEAD_DIM), pos)
    a_v = a_v.reshape(b, t, N_KV_HEADS, HEAD_DIM)
    i_q = _rotary(i_q.reshape(b, t, N_IDX_HEADS, IDX_HEAD_DIM), pos)
    i_k = _rotary(i_k.reshape(b, t, 1, IDX_HEAD_DIM), pos)[:, :, 0]
    i_w = i_w * IDX_SCALE
    n_keys = past + t
    tq, tk = attn_tiles
    lp = -(-n_keys // tk) * tk

    def keys_cat(old, new, width):
        allk = jnp.concatenate([old.reshape(b, past, width).astype(BF16), new.reshape(b, t, width).astype(BF16)], axis=1)
        return jnp.pad(allk, ((0, 0), (0, lp - n_keys), (0, 0)))

    attn = dsa_attention(
        jnp.transpose(i_q, (0, 2, 1, 3)).astype(BF16), i_w, keys_cat(past_ik, i_k, IDX_HEAD_DIM),
        a_q.reshape(b, t, ATTN_WIDTH).astype(BF16), keys_cat(past_k, a_k, KV_WIDTH), keys_cat(past_v, a_v, KV_WIDTH),
        n_keys=n_keys, past=past, tq=tq, tk=tk)

    qkv = jnp.concatenate([d_q, d_k, d_v], axis=-1)
    ext = jnp.concatenate([dn_buf, qkv], axis=1)
    conv = ext[:, 0:t] * dn_conv_w[0]
    for i in range(1, DN_CONV):
        conv = conv + ext[:, i:i + t] * dn_conv_w[i]
    new_dn_buf = ext[:, t:]
    conv = jax.nn.silu(conv)
    c_q, c_k, c_v = jnp.split(conv, 3, axis=-1)
    q_dn = _l2n(c_q.reshape(b, t, DN_HEADS, DN_HEAD_DIM)).reshape(b, t, DN_WIDTH)
    k_dn = _l2n(c_k.reshape(b, t, DN_HEADS, DN_HEAD_DIM)).reshape(b, t, DN_WIDTH)
    beta = jax.nn.sigmoid(d_b)
    g = -jnp.exp(dn_a_log) * jax.nn.softplus(d_a + dn_dt_bias)
    chunk = CHUNK if t % CHUNK == 0 else t
    gc = jnp.cumsum(g.reshape(b, t // chunk, chunk, DN_HEADS), axis=2).reshape(b, t, DN_HEADS)
    o_dn, new_state = delta_rule(q_dn, k_dn, c_v, d_z, gc, beta, dn_norm_g, dn_state, chunk)

    mix = jnp.concatenate([attn, o_dn], axis=-1).reshape(b * t, -1)
    x2 = matmul_res(mix, w_out, x2)

    f2 = ffn_w_up.shape[1]
    xt = jnp.swapaxes(x2.reshape(b, t, d), 0, 1).reshape(t * b, d)
    halo = max(SUBLANES, (FFN_CONV - 1) * b)
    hist = jnp.swapaxes(ffn_buf, 0, 1).reshape((FFN_CONV - 1) * b, f2)
    hist = jnp.pad(hist, ((halo - hist.shape[0], 0), (0, 0)))
    act, tail = ffn_up(xt, norm2_g, ffn_w_up, ffn_conv_w, hist, step=b)
    new_ffn_buf = jnp.swapaxes(tail[halo - (FFN_CONV - 1) * b:].reshape(FFN_CONV - 1, b, f2), 0, 1)
    act = jnp.swapaxes(act.reshape(t, b, -1), 0, 1).reshape(b * t, -1)
    x2 = matmul_res(act, ffn_w_down, x2)
    return x2.reshape(b, t, d), a_k, a_v, i_k, new_dn_buf, new_state, new_ffn_buf


def kernel(x_prompt, x_sample, cache_k, cache_v, cache_kidx, state_dn_conv, state_dn, state_ffn_conv,
           norm1_g, w_in, dn_conv_w, dn_a_log, dn_dt_bias, dn_norm_g, w_out, norm2_g,
           ffn_w_up, ffn_conv_w, ffn_w_down, final_g):
    depth = w_in.shape[0]
    bp, tp, d = x_prompt.shape
    bs, ts, _ = x_sample.shape
    f2 = ffn_w_up.shape[2]
    zk = jnp.zeros((bp, 0, N_KV_HEADS, HEAD_DIM), F32)
    zik = jnp.zeros((bp, 0, IDX_HEAD_DIM), F32)
    z_dn_buf = jnp.zeros((bp, DN_CONV - 1, 3 * DN_WIDTH), F32)
    z_dn_state = jnp.zeros((bp, DN_HEADS, DN_HEAD_DIM, DN_HEAD_DIM), F32)
    z_ffn_buf = jnp.zeros((bp, FFN_CONV - 1, f2), F32)

    tq_p = _pick(tp, (128, 64, 32, 16))
    tk_p = _pick(tp, (512, 256, 128))
    n_keys_s = cache_k.shape[2] + ts
    tk_s = -(-n_keys_s // LANES) * LANES

    xp, xs = x_prompt, x_sample
    st_p, st_s = [], []
    for l in range(depth):
        lw = (norm1_g[l], w_in[l].astype(BF16), dn_conv_w[l], dn_a_log[l], dn_dt_bias[l], dn_norm_g[l],
              w_out[l].astype(BF16), norm2_g[l], ffn_w_up[l].astype(BF16), ffn_conv_w[l],
              ffn_w_down[l].astype(BF16))
        xp, *sp = _layer(xp, zk, zk, zik, z_dn_buf, z_dn_state, z_ffn_buf, lw, attn_tiles=(tq_p, tk_p))
        xs, *ss = _layer(xs, cache_k[l], cache_v[l], cache_kidx[l], state_dn_conv[l], state_dn[l],
                         state_ffn_conv[l], lw, attn_tiles=(ts, tk_s))
        st_p.append(sp)
        st_s.append(ss)

    y_prompt = rms_norm_rows(xp.reshape(bp * tp, d), final_g).reshape(bp, tp, d)
    y_sample = rms_norm_rows(xs.reshape(bs * ts, d), final_g).reshape(bs, ts, d)

    def stacked(states, i):
        return jnp.stack([s[i] for s in states], axis=0)

    return (y_prompt, y_sample,
            *[stacked(st_p, i) for i in range(6)],
            *[stacked(st_s, i) for i in range(6)])
```

```python
import functools

import numpy as np
import jax
import jax.numpy as jnp
from jax import lax
from jax.experimental import pallas as pl
from jax.experimental.pallas import tpu as pltpu

F32 = jnp.float32
BF16 = jnp.bfloat16
I32 = jnp.int32

CHUNK = 64
HEAD_DIM = 128
N_HEADS = 16
N_KV_HEADS = 4
KV_GROUP = N_HEADS // N_KV_HEADS
ATTN_WIDTH = N_HEADS * HEAD_DIM
KV_WIDTH = N_KV_HEADS * HEAD_DIM
N_IDX_HEADS = 32
IDX_HEAD_DIM = 64
IDX_WIDTH = N_IDX_HEADS * IDX_HEAD_DIM
IDX_SCALE = IDX_WIDTH ** -0.5
IDX_TOPK_MAX = 256
DN_HEADS = 16
DN_HEAD_DIM = 128
DN_WIDTH = DN_HEADS * DN_HEAD_DIM
DN_CONV = 4
FFN_CONV = 3
ROPE_THETA = 10000.0
EPS = 1e-6
LOG2E = 1.4426950408889634

LANES = 128
SUBLANES = 8
VMEM_LIMIT = 56 * 1024 * 1024

INT_MIN = -(2 ** 31)
INT_MAX = 2 ** 31 - 1
NEG_BIG = -1e30

COL_AQ, COL_IQ, COL_DQ, COL_DK, COL_DV, COL_DZ = (i * ATTN_WIDTH for i in range(6))
COL_AK = 6 * ATTN_WIDTH
COL_AV = COL_AK + KV_WIDTH
COL_SMALL = COL_AV + KV_WIDTH
SMALL_IK, SMALL_IW, SMALL_DB, SMALL_DA = 0, 64, 96, 112
IN_COLS = COL_SMALL + LANES
assert ATTN_WIDTH == IDX_WIDTH == DN_WIDTH


def _in_proj_perm():
    splits = (ATTN_WIDTH, KV_WIDTH, KV_WIDTH, IDX_WIDTH, IDX_HEAD_DIM, N_IDX_HEADS,
              DN_WIDTH, DN_WIDTH, DN_WIDTH, DN_HEADS, DN_HEADS, DN_WIDTH)
    starts = np.concatenate([[0], np.cumsum(splits)])
    seg = {n: np.arange(starts[i], starts[i + 1]) for i, n in enumerate(
        ("a_q", "a_k", "a_v", "i_q", "i_k", "i_w", "d_q", "d_k", "d_v", "d_b", "d_a", "d_z"))}
    order = ("a_q", "i_q", "d_q", "d_k", "d_v", "d_z", "a_k", "a_v", "i_k", "i_w", "d_b", "d_a")
    return np.concatenate([seg[n] for n in order])


def _cparams(sem):
    return pltpu.CompilerParams(dimension_semantics=sem, vmem_limit_bytes=VMEM_LIMIT)


def _pick(n, prefs):
    for p in prefs:
        if n % p == 0:
            return p
    raise ValueError(f"no tile in {prefs} divides {n}")


ROW_TILES = (512, 256, 128, 64, 32, 16, 8)


def _norm_matmul_kernel(x_ref, g_ref, w_ref, o_ref, h_ref):
    @pl.when(pl.program_id(1) == 0)
    def _():
        x = x_ref[...]
        ms = jnp.mean(x * x, axis=-1, keepdims=True)
        h_ref[...] = (x * lax.rsqrt(ms + EPS) * g_ref[...]).astype(BF16)

    o_ref[...] = jnp.dot(h_ref[...], w_ref[...], preferred_element_type=F32)


def norm_matmul(x, g, w):
    n, d = x.shape
    cols = w.shape[1]
    tm = _pick(n, ROW_TILES)
    tn = _pick(cols, (640, 512, 256, 128))
    return pl.pallas_call(
        _norm_matmul_kernel,
        grid=(n // tm, cols // tn),
        in_specs=[pl.BlockSpec((tm, d), lambda i, j: (i, 0)),
                  pl.BlockSpec((1, d), lambda i, j: (0, 0)),
                  pl.BlockSpec((d, tn), lambda i, j: (0, j))],
        out_specs=pl.BlockSpec((tm, tn), lambda i, j: (i, j)),
        out_shape=jax.ShapeDtypeStruct((n, cols), F32),
        scratch_shapes=[pltpu.VMEM((tm, d), BF16)],
        compiler_params=_cparams(("parallel", "arbitrary")),
        name="norm_matmul",
    )(x, g.reshape(1, d), w)


def _matmul_res_kernel(*refs, n_lhs):
    a_refs, w_refs, x_ref, o_ref = refs[:n_lhs], refs[n_lhs:2 * n_lhs], refs[2 * n_lhs], refs[2 * n_lhs + 1]
    acc = x_ref[...]
    for a_ref, w_ref in zip(a_refs, w_refs):
        acc = acc + jnp.dot(a_ref[...], w_ref[...], preferred_element_type=F32)
    o_ref[...] = acc


def matmul_res(lhs, w, x):
    n, k = lhs[0].shape
    cols = w.shape[1]
    tm = _pick(n, ROW_TILES)
    tn = _pick(cols, (256,) if k > 8192 else (1024, 512, 256, 128))
    kern = functools.partial(_matmul_res_kernel, n_lhs=len(lhs))
    return pl.pallas_call(
        kern,
        grid=(n // tm, cols // tn),
        in_specs=([pl.BlockSpec((tm, k), lambda i, j: (i, 0)) for _ in lhs]
                  + [pl.BlockSpec((k, tn), lambda i, j, r=r: (r, j)) for r in range(len(lhs))]
                  + [pl.BlockSpec((tm, tn), lambda i, j: (i, j))]),
        out_specs=pl.BlockSpec((tm, tn), lambda i, j: (i, j)),
        out_shape=jax.ShapeDtypeStruct((n, cols), F32),
        compiler_params=_cparams(("parallel", "arbitrary")),
        name="matmul_res",
    )(*lhs, *([w] * len(lhs)), x)


def _rms_norm_kernel(x_ref, g_ref, o_ref):
    x = x_ref[...]
    ms = jnp.mean(x * x, axis=-1, keepdims=True)
    o_ref[...] = x * lax.rsqrt(ms + EPS) * g_ref[...]


def rms_norm_rows(x, g):
    n, d = x.shape
    tm = _pick(n, ROW_TILES)
    return pl.pallas_call(
        _rms_norm_kernel,
        grid=(n // tm,),
        in_specs=[pl.BlockSpec((tm, d), lambda i: (i, 0)), pl.BlockSpec((1, d), lambda i: (0, 0))],
        out_specs=pl.BlockSpec((tm, d), lambda i: (i, 0)),
        out_shape=jax.ShapeDtypeStruct((n, d), F32),
        compiler_params=_cparams(("parallel",)),
        name="rms_norm",
    )(x, g.reshape(1, d))


FFN_TILE = 256


def ffn_interleave(a):
    f = a.shape[-1] // 2
    lead = a.shape[:-1]
    return jnp.swapaxes(a.reshape(*lead, 2, f // FFN_TILE, FFN_TILE), -3, -2).reshape(*lead, 2 * f)


def ffn_deinterleave(a):
    f = a.shape[-1] // 2
    lead = a.shape[:-1]
    return jnp.swapaxes(a.reshape(*lead, f // FFN_TILE, 2, FFN_TILE), -3, -2).reshape(*lead, 2 * f)


def _ffn_up_kernel(x_ref, g_ref, w_ref, c_ref, hist_ref, a_ref, st_ref, h_ref, e_ref, cy_ref, *, tm, tn, step, halo):
    i = pl.program_id(0)
    j = pl.program_id(1)

    @pl.when(j == 0)
    def _():
        x = x_ref[...]
        ms = jnp.mean(x * x, axis=-1, keepdims=True)
        h_ref[...] = (x * lax.rsqrt(ms + EPS) * g_ref[...]).astype(BF16)

    up = jnp.dot(h_ref[...], w_ref[...], preferred_element_type=F32)
    e_ref[halo:halo + tm, :] = up

    @pl.when(i == 0)
    def _():
        e_ref[0:halo, :] = hist_ref[...]

    @pl.when(i > 0)
    def _():
        e_ref[0:halo, :] = cy_ref[j]

    c = c_ref[...]
    y = (e_ref[halo - 2 * step:halo - 2 * step + tm, :] * c[0:1, :]
         + e_ref[halo - step:halo - step + tm, :] * c[1:2, :]
         + up * c[2:3, :])
    tail = e_ref[tm:tm + halo, :]
    cy_ref[j] = tail
    st_ref[...] = tail
    yg = y[:, :tn]
    a_ref[...] = (yg * jax.nn.sigmoid(yg) * y[:, tn:]).astype(BF16)


def ffn_up(x, g, w_up, conv_w, hist, step):
    n, d = x.shape
    f2 = w_up.shape[1]
    f = f2 // 2
    halo = hist.shape[0]
    tm = _pick(n, ROW_TILES)
    tn = FFN_TILE
    nj = f // tn
    kern = functools.partial(_ffn_up_kernel, tm=tm, tn=tn, step=step, halo=halo)
    act, tail = pl.pallas_call(
        kern,
        grid=(n // tm, nj),
        in_specs=[pl.BlockSpec((tm, d), lambda i, j: (i, 0)),
                  pl.BlockSpec((1, d), lambda i, j: (0, 0)),
                  pl.BlockSpec((d, 2 * tn), lambda i, j: (0, j)),
                  pl.BlockSpec((FFN_CONV, 2 * tn), lambda i, j: (0, j)),
                  pl.BlockSpec((halo, 2 * tn), lambda i, j: (0, j))],
        out_specs=[pl.BlockSpec((tm, tn), lambda i, j: (i, j)),
                   pl.BlockSpec((None, halo, 2 * tn), lambda i, j: (i, 0, j))],
        out_shape=[jax.ShapeDtypeStruct((n, f), BF16),
                   jax.ShapeDtypeStruct((n // tm, halo, f2), F32)],
        scratch_shapes=[pltpu.VMEM((tm, d), BF16),
                        pltpu.VMEM((tm + halo, 2 * tn), F32),
                        pltpu.VMEM((nj, halo, 2 * tn), F32)],
        compiler_params=_cparams(("arbitrary", "arbitrary")),
        name="ffn_up",
    )(x, g.reshape(1, d), w_up, conv_w, hist)
    return act, tail[-1]


def _attn_prep_kernel(aq_ref, iq_ref, ak_ref, av_ref, sm_ref, cosa_ref, sina_ref, cosi_ref, sini_ref,
                      q_ref, iqo_ref, kf_ref, kb_ref, vb_ref, ik2_ref, smo_ref, *, tm):
    cosa, sina = cosa_ref[...], sina_ref[...]
    cosi, sini = cosi_ref[...], sini_ref[...]
    lane = lax.broadcasted_iota(I32, (tm, LANES), 1)
    first_half = (lane & (IDX_HEAD_DIM - 1)) < IDX_HEAD_DIM // 2

    def rot_head(x):
        return x * cosa + pltpu.roll(x, HEAD_DIM // 2, 1) * sina

    def rot_idx(x):
        partner = jnp.where(first_half, pltpu.roll(x, LANES - IDX_HEAD_DIM // 2, 1), pltpu.roll(x, IDX_HEAD_DIM // 2, 1))
        return x * cosi + partner * sini

    qscale = (HEAD_DIM ** -0.5) * LOG2E
    for h in range(N_HEADS):
        sl = slice(h * LANES, (h + 1) * LANES)
        q_ref[:, sl] = (rot_head(aq_ref[:, sl]) * qscale).astype(BF16)
        iqo_ref[:, sl] = rot_idx(iq_ref[:, sl]).astype(BF16)
    for n in range(N_KV_HEADS):
        sl = slice(n * LANES, (n + 1) * LANES)
        kr = rot_head(ak_ref[:, sl])
        kf_ref[:, sl] = kr
        kb_ref[:, sl] = kr.astype(BF16)
    vb_ref[...] = av_ref[...].astype(BF16)
    sm = sm_ref[...]
    ik = jnp.where(lane < SMALL_IW, rot_idx(sm), 0.0)
    smo_ref[...] = jnp.where(lane < SMALL_IW, ik, jnp.where(lane < SMALL_DB, sm * IDX_SCALE, sm))
    ik2_ref[:, :LANES] = ik.astype(BF16)
    ik2_ref[:, LANES:] = pltpu.roll(ik, IDX_HEAD_DIM, 1).astype(BF16)


def attn_prep(proj, pos):
    n = proj.shape[0]
    tm = _pick(n, (256, 128, 64, 32, 16, 8))
    posf = pos.astype(F32)[:, None]

    def tables(half):
        inv_freq = jnp.float32(ROPE_THETA) ** (-jnp.arange(half, dtype=F32) / half)
        ang = posf * inv_freq[None, :]
        cos, sin = jnp.cos(ang), jnp.sin(ang)
        rep = LANES // (2 * half)
        return jnp.tile(jnp.concatenate([cos, cos], axis=1), (1, rep)), jnp.tile(jnp.concatenate([-sin, sin], axis=1), (1, rep))

    cosa, sina = tables(HEAD_DIM // 2)
    cosi, sini = tables(IDX_HEAD_DIM // 2)
    wide = lambda c: pl.BlockSpec((tm, ATTN_WIDTH), lambda i, c=c: (i, c // ATTN_WIDTH))
    kvw = lambda c: pl.BlockSpec((tm, KV_WIDTH), lambda i, c=c: (i, c // KV_WIDTH))
    tab = pl.BlockSpec((tm, LANES), lambda i: (i, 0))
    row = lambda w: pl.BlockSpec((tm, w), lambda i: (i, 0))
    return pl.pallas_call(
        functools.partial(_attn_prep_kernel, tm=tm),
        grid=(n // tm,),
        in_specs=[wide(COL_AQ), wide(COL_IQ), kvw(COL_AK), kvw(COL_AV),
                  pl.BlockSpec((tm, LANES), lambda i: (i, COL_SMALL // LANES)), tab, tab, tab, tab],
        out_specs=[row(ATTN_WIDTH), row(IDX_WIDTH), row(KV_WIDTH), row(KV_WIDTH), row(KV_WIDTH), row(2 * LANES), row(LANES)],
        out_shape=[jax.ShapeDtypeStruct((n, ATTN_WIDTH), BF16),
                   jax.ShapeDtypeStruct((n, IDX_WIDTH), BF16),
                   jax.ShapeDtypeStruct((n, KV_WIDTH), F32),
                   jax.ShapeDtypeStruct((n, KV_WIDTH), BF16),
                   jax.ShapeDtypeStruct((n, KV_WIDTH), BF16),
                   jax.ShapeDtypeStruct((n, 2 * LANES), BF16),
                   jax.ShapeDtypeStruct((n, LANES), F32)],
        compiler_params=_cparams(("parallel",)),
        name="attn_prep",
    )(proj, proj, proj, proj, proj, cosa, sina, cosi, sini)


IDX_PAIR_GROUP = 4
N_IDX_PAIRS = N_IDX_HEADS // 2


def _dsa_kernel(tb_ref, tqb_ref, tph_ref, tkb_ref, tnkb_ref, tikb_ref, tkvb_ref,
                iq_ref, iw_ref, ik_ref, q_ref, k_ref, v_ref, o_ref,
                keys_ref, sc_ref, thr_ref, pcut_ref, wb_ref, iqs_ref, qs_ref, m_ref, acc_ref,
                *, tq, tk, n_keys, past, topk, pos_bits):
    s = pl.program_id(0)
    ph = tph_ref[s]
    kb = tkb_ref[s]
    nkb = tnkb_ref[s]
    q0 = tqb_ref[s] * tq + past
    ncol = tk // LANES
    grows = IDX_PAIR_GROUP * tq

    @pl.when(ph == 0)
    def _scores():
        @pl.when(kb == 0)
        def _():
            w = iw_ref[0]
            for h in range(N_IDX_HEADS):
                wb_ref[h] = jnp.broadcast_to(w[:, h:h + 1], (tq, LANES))
            for p in range(N_IDX_PAIRS):
                iqs_ref[p * tq:(p + 1) * tq, :] = iq_ref[0, :, p * LANES:(p + 1) * LANES]

        ik2 = ik_ref[0]
        ik_even, ik_odd = ik2[:, :LANES], ik2[:, LANES:]
        sc_ref[...] = jnp.zeros((tq, tk), F32)

        def group(gi, carry):
            lhs = iqs_ref[pl.ds(pl.multiple_of(gi * grows, grows), grows), :]
            raws = [lax.dot_general(lhs, kk, (((1,), (1,)), ((), ())), preferred_element_type=F32)
                    for kk in (ik_even, ik_odd)]
            part = None
            for j in range(IDX_PAIR_GROUP):
                for par in range(2):
                    wv = wb_ref[(gi * IDX_PAIR_GROUP + j) * 2 + par]
                    wv = jnp.concatenate([wv] * ncol, axis=1) if ncol > 1 else wv
                    term = wv * jnp.maximum(raws[par][j * tq:(j + 1) * tq, :], 0.0)
                    part = term if part is None else part + term
            sc_ref[...] += part
            return carry

        lax.fori_loop(0, N_IDX_PAIRS // IDX_PAIR_GROUP, group, 0)

        sc = sc_ref[...] + 0.0
        bits = pltpu.bitcast(sc, I32)
        key = bits ^ ((bits >> 31) & INT_MAX)
        kpos = kb * tk + lax.broadcasted_iota(I32, (tq, tk), 1)
        qpos = q0 + lax.broadcasted_iota(I32, (tq, tk), 0)
        adm = jnp.logical_and((kpos >> 6) <= (qpos >> 6), kpos < n_keys)
        keys_ref[kb] = jnp.where(adm, key, INT_MIN)

    @pl.when(ph == 1)
    def _attend():
        def count_rows(pred):
            def blk(b, cnt):
                hit = pred(keys_ref[b], b).astype(I32)
                for c in range(ncol):
                    cnt = cnt + hit[:, c * LANES:(c + 1) * LANES]
                return cnt

            cnt = lax.fori_loop(0, nkb, blk, jnp.zeros((tq, LANES), I32))
            return jnp.sum(cnt, axis=1, keepdims=True)

        @pl.when(kb == 0)
        def _select():
            def bit_body(it, thr):
                cand = thr + lax.shift_left(jnp.int32(1), 31 - it)
                total = count_rows(lambda x, b: x >= cand)
                return jnp.where(total >= topk, cand, thr)

            thr = lax.fori_loop(0, 32, bit_body, jnp.full((tq, 1), INT_MIN, I32))
            thr = jnp.maximum(thr, INT_MIN + 1)
            thr_ref[...] = thr
            pcut_ref[...] = jnp.full((tq, 1), INT_MAX, I32)

            tied = count_rows(lambda x, b: x >= thr) > topk

            @pl.when(jnp.max(tied.astype(I32)) > 0)
            def _ties():
                room = topk - count_rows(lambda x, b: x > thr)

                def pos_body(it, cut):
                    cand = cut + lax.shift_left(jnp.int32(1), pos_bits - 1 - it)

                    def pred(x, b):
                        kpos = b * tk + lax.broadcasted_iota(I32, (tq, tk), 1)
                        return jnp.logical_and(x == thr, kpos < cand)

                    return jnp.where(count_rows(pred) < room, cand, cut)

                cut = lax.fori_loop(0, pos_bits, pos_body, jnp.zeros((tq, 1), I32))
                pcut_ref[...] = jnp.where(tied, cut, INT_MAX)

            m_ref[...] = jnp.full(m_ref.shape, NEG_BIG, F32)
            acc_ref[...] = jnp.zeros(acc_ref.shape, F32)
            qv = q_ref[0]
            for n in range(N_KV_HEADS):
                for g in range(KV_GROUP):
                    hd = n * KV_GROUP + g
                    qs_ref[n, g * tq:(g + 1) * tq, :] = qv[:, hd * HEAD_DIM:(hd + 1) * HEAD_DIM]

        key = keys_ref[kb]
        thr = thr_ref[...]
        kpos = kb * tk + lax.broadcasted_iota(I32, (tq, tk), 1)
        keep = jnp.logical_or(key > thr, jnp.logical_and(key == thr, kpos <= pcut_ref[...]))
        bias = jnp.where(keep, 0.0, NEG_BIG).astype(F32)
        bias4 = jnp.concatenate([bias] * KV_GROUP, axis=0)
        kv = k_ref[0]
        vv = v_ref[0]
        ones = jnp.ones((tk, HEAD_DIM), BF16)
        for n in range(N_KV_HEADS):
            kn = kv[:, n * HEAD_DIM:(n + 1) * HEAD_DIM]
            ve = jnp.concatenate([vv[:, n * HEAD_DIM:(n + 1) * HEAD_DIM], ones], axis=1)
            lg = lax.dot_general(qs_ref[n], kn, (((1,), (1,)), ((), ())), preferred_element_type=F32) + bias4
            cols = [lg[:, c * LANES:(c + 1) * LANES] for c in range(ncol)]
            mx = cols[0]
            for c in range(1, ncol):
                mx = jnp.maximum(mx, cols[c])
            m_prev = m_ref[n]
            m_new = jnp.maximum(m_prev, jnp.max(mx, axis=1, keepdims=True))
            alpha = jnp.exp2(m_prev - m_new)
            p = jnp.concatenate([jnp.exp2(cols[c] - m_new).astype(BF16) for c in range(ncol)], axis=1)
            pv = jnp.dot(p, ve, preferred_element_type=F32)
            acc_ref[n] = jnp.concatenate([alpha, alpha], axis=1) * acc_ref[n] + pv
            m_ref[n] = m_new

        @pl.when(kb == nkb - 1)
        def _finish():
            for n in range(N_KV_HEADS):
                a = acc_ref[n]
                o = a[:, :HEAD_DIM] / a[:, HEAD_DIM:]
                for g in range(KV_GROUP):
                    hd = n * KV_GROUP + g
                    o_ref[0, :, hd * HEAD_DIM:(hd + 1) * HEAD_DIM] = o[g * tq:(g + 1) * tq, :].astype(BF16)


def _dsa_tables(batch, t, tq, tk, past, n_keys):
    rows = []
    for b in range(batch):
        for qb in range(t // tq):
            last_q = past + qb * tq + tq - 1
            limit = min(n_keys, (last_q // CHUNK + 1) * CHUNK)
            nkb = -(-limit // tk)
            for ph in range(2):
                for kb in range(nkb):
                    ikb = kb if ph == 0 else nkb - 1
                    kvb = 0 if ph == 0 else kb
                    rows.append((b, qb, ph, kb, nkb, ikb, kvb))
    tab = np.asarray(rows, dtype=np.int32).T
    return [jnp.asarray(tab[r]) for r in range(tab.shape[0])]


def dsa_attention(iq, iw, ik2, q, k, v, *, n_keys, past, tq, tk):
    batch, t, _ = iq.shape
    lp = ik2.shape[1]
    assert t % tq == 0 and lp % tk == 0 and tk % LANES == 0
    topk = min(IDX_TOPK_MAX, n_keys // 4)
    tables = _dsa_tables(batch, t, tq, tk, past, n_keys)
    nsteps = int(tables[0].shape[0])
    nkb_max = lp // tk
    kern = functools.partial(_dsa_kernel, tq=tq, tk=tk, n_keys=n_keys, past=past, topk=topk,
                             pos_bits=max(1, (lp - 1).bit_length()))
    qmap = lambda s, tb, tqb, tph, tkb, tnkb, tikb, tkvb: (tb[s], tqb[s], 0)
    grid_spec = pltpu.PrefetchScalarGridSpec(
        num_scalar_prefetch=7,
        grid=(nsteps,),
        in_specs=[
            pl.BlockSpec((1, tq, IDX_WIDTH), qmap),
            pl.BlockSpec((1, tq, N_IDX_HEADS), qmap),
            pl.BlockSpec((1, tk, 2 * LANES), lambda s, tb, tqb, tph, tkb, tnkb, tikb, tkvb: (tb[s], tikb[s], 0)),
            pl.BlockSpec((1, tq, ATTN_WIDTH), qmap),
            pl.BlockSpec((1, tk, KV_WIDTH), lambda s, tb, tqb, tph, tkb, tnkb, tikb, tkvb: (tb[s], tkvb[s], 0)),
            pl.BlockSpec((1, tk, KV_WIDTH), lambda s, tb, tqb, tph, tkb, tnkb, tikb, tkvb: (tb[s], tkvb[s], 0)),
        ],
        out_specs=pl.BlockSpec((1, tq, ATTN_WIDTH), qmap),
        scratch_shapes=[
            pltpu.VMEM((nkb_max, tq, tk), I32),
            pltpu.VMEM((tq, tk), F32),
            pltpu.VMEM((tq, 1), I32),
            pltpu.VMEM((tq, 1), I32),
            pltpu.VMEM((N_IDX_HEADS, tq, LANES), F32),
            pltpu.VMEM((N_IDX_PAIRS * tq, LANES), BF16),
            pltpu.VMEM((N_KV_HEADS, KV_GROUP * tq, HEAD_DIM), BF16),
            pltpu.VMEM((N_KV_HEADS, KV_GROUP * tq, LANES), F32),
            pltpu.VMEM((N_KV_HEADS, KV_GROUP * tq, 2 * HEAD_DIM), F32),
        ],
    )
    return pl.pallas_call(
        kern,
        grid_spec=grid_spec,
        out_shape=jax.ShapeDtypeStruct((batch, t, ATTN_WIDTH), BF16),
        compiler_params=_cparams(("arbitrary",)),
        name="dsa_attention",
    )(*tables, iq, iw, ik2, q, k, v)


def _mm(a, b):
    return jnp.dot(a.astype(BF16), b.astype(BF16), preferred_element_type=F32)


def _mm_nt(a, b):
    return lax.dot_general(a.astype(BF16), b.astype(BF16), (((1,), (1,)), ((), ())), preferred_element_type=F32)


def _mm_tn(a, b):
    return lax.dot_general(a.astype(BF16), b.astype(BF16), (((0,), (0,)), ((), ())), preferred_element_type=F32)


def _delta_kernel(dq_ref, dk_ref, dv_ref, z_ref, bq_ref, bk_ref, bv_ref, cq_ref, ck_ref, cv_ref,
                  gc_ref, gr_ref, b_ref, gn_ref, s0_ref, o_ref, s_ref, eq_ref, ek_ref, ev_ref,
                  *, chunk, n_double):
    first = pl.program_id(1) == 0

    @pl.when(first)
    def _():
        s_ref[...] = s0_ref[...]

    def conv_silu(x_ref, buf_ref, cw_ref, e_ref):
        @pl.when(first)
        def _():
            e_ref[0:SUBLANES, :] = buf_ref[0]

        e_ref[SUBLANES:SUBLANES + chunk, :] = x_ref[0]
        w = cw_ref[...]
        y = e_ref[SUBLANES:SUBLANES + chunk, :] * w[DN_CONV - 1:DN_CONV, :]
        for i in range(DN_CONV - 1):
            off = SUBLANES - (DN_CONV - 1) + i
            y = y + e_ref[off:off + chunk, :] * w[i:i + 1, :]
        e_ref[0:SUBLANES, :] = e_ref[chunk:chunk + SUBLANES, :]
        return y * jax.nn.sigmoid(y)

    yq = conv_silu(dq_ref, bq_ref, cq_ref, eq_ref)
    yk = conv_silu(dk_ref, bk_ref, ck_ref, ek_ref)
    yv = conv_silu(dv_ref, bv_ref, cv_ref, ev_ref)

    row = lax.broadcasted_iota(I32, (chunk, chunk), 0)
    col = lax.broadcasted_iota(I32, (chunk, chunk), 1)
    eye = (row == col).astype(F32)
    gcs = gc_ref[0]
    grs = gr_ref[0, 0]
    betas = b_ref[0]
    gn = gn_ref[...]
    hs = range(DN_HEADS)
    sl = [slice(h * DN_HEAD_DIM, (h + 1) * DN_HEAD_DIM) for h in hs]

    def l2n(x):
        return x * lax.rsqrt(jnp.sum(x * x, axis=-1, keepdims=True) + EPS)

    kh = [l2n(yk[:, sl[h]]) for h in hs]
    qh = [l2n(yq[:, sl[h]]) * (DN_HEAD_DIM ** -0.5) for h in hs]
    gc = [gcs[:, h:h + 1] for h in hs]
    beta = [betas[:, h:h + 1] for h in hs]
    decay = [jnp.exp(jnp.where(row >= col, gc[h] - grs[h:h + 1, :], -jnp.inf)) for h in hs]
    kk = [_mm_nt(kh[h], kh[h]) for h in hs]
    qk = [_mm_nt(qh[h], kh[h]) * decay[h] for h in hs]
    x = [-jnp.where(row > col, beta[h] * kk[h] * decay[h], 0.0) for h in hs]
    parts = [x]
    for _ in range(n_double):
        x = [_mm(x[h], x[h]) for h in hs]
        parts.append(x)
    while len(parts) > 1:
        nxt = [[parts[i][h] + parts[i + 1][h] + _mm(parts[i][h], parts[i + 1][h]) for h in hs]
               for i in range(0, len(parts) - 1, 2)]
        if len(parts) % 2:
            nxt.append(parts[-1])
        parts = nxt
    t = [eye + parts[0][h] for h in hs]
    eg = [jnp.exp(gc[h]) for h in hs]
    uw = [_mm(t[h], jnp.concatenate([yv[:, sl[h]] * beta[h], kh[h] * (beta[h] * eg[h])], axis=1)) for h in hs]
    s = [s_ref[0, h] for h in hs]
    ws_qs = [_mm(jnp.concatenate([uw[h][:, DN_HEAD_DIM:], qh[h] * eg[h]], axis=0), s[h]) for h in hs]
    v_new = [uw[h][:, :DN_HEAD_DIM] - ws_qs[h][:chunk] for h in hs]
    gl = [gc[h][chunk - 1:chunk, :] for h in hs]
    o = [ws_qs[h][chunk:] + _mm(qk[h], v_new[h]) for h in hs]
    upd = [_mm_tn(kh[h] * jnp.exp(gl[h] - gc[h]), v_new[h]) for h in hs]
    for h in hs:
        s_ref[0, h] = s[h] * jnp.exp(gl[h]) + upd[h]
        ms = jnp.mean(o[h] * o[h], axis=-1, keepdims=True)
        on = o[h] * lax.rsqrt(ms + EPS) * gn
        z = z_ref[0, :, sl[h]]
        o_ref[0, :, sl[h]] = (on * (z * jax.nn.sigmoid(z))).astype(BF16)


def delta_rule(proj, dn_buf, conv_w, gc, beta, gnorm, state, chunk):
    batch, t, _ = proj.shape
    n = t // chunk
    n_double = max(0, (chunk - 1).bit_length() - 1)
    gr = jnp.swapaxes(gc.reshape(batch, n, chunk, DN_HEADS), 2, 3)
    buf8 = jnp.pad(dn_buf, ((0, 0), (SUBLANES - (DN_CONV - 1), 0), (0, 0)))
    kern = functools.partial(_delta_kernel, chunk=chunk, n_double=n_double)
    pcol = lambda c: pl.BlockSpec((1, chunk, DN_WIDTH), lambda b, i, c=c: (b, i, c // DN_WIDTH))
    bcol = lambda j: pl.BlockSpec((1, SUBLANES, DN_WIDTH), lambda b, i, j=j: (b, 0, j))
    wcol = lambda j: pl.BlockSpec((DN_CONV, DN_WIDTH), lambda b, i, j=j: (0, j))
    wide = pl.BlockSpec((1, chunk, DN_WIDTH), lambda b, i: (b, i, 0))
    narrow = pl.BlockSpec((1, chunk, DN_HEADS), lambda b, i: (b, i, 0))
    st = pl.BlockSpec((1, DN_HEADS, DN_HEAD_DIM, DN_HEAD_DIM), lambda b, i: (b, 0, 0, 0))
    return pl.pallas_call(
        kern,
        grid=(batch, n),
        in_specs=[pcol(COL_DQ), pcol(COL_DK), pcol(COL_DV), pcol(COL_DZ),
                  bcol(0), bcol(1), bcol(2), wcol(0), wcol(1), wcol(2),
                  narrow,
                  pl.BlockSpec((1, 1, DN_HEADS, chunk), lambda b, i: (b, i, 0, 0)),
                  narrow,
                  pl.BlockSpec((1, DN_HEAD_DIM), lambda b, i: (0, 0)),
                  st],
        out_specs=[wide, st],
        out_shape=[jax.ShapeDtypeStruct((batch, t, DN_WIDTH), BF16),
                   jax.ShapeDtypeStruct(state.shape, F32)],
        scratch_shapes=[pltpu.VMEM((chunk + SUBLANES, DN_WIDTH), F32)] * 3,
        compiler_params=_cparams(("parallel", "arbitrary")),
        name="delta_rule",
    )(proj, proj, proj, proj, buf8, buf8, buf8, conv_w, conv_w, conv_w,
      gc, gr, beta, gnorm.reshape(1, DN_HEAD_DIM), state)


def _layer(x, past_k, past_v, past_ik, dn_buf, dn_state, ffn_buf, lw, *, attn_tiles):
    (norm1_g, w_in, dn_conv_w, dn_a_log, dn_dt_bias, dn_norm_g, w_out, norm2_g, ffn_w_up, ffn_conv_w,
     ffn_w_down) = lw
    b, t, d = x.shape
    past = past_k.shape[1]
    n_keys = past + t
    x2 = x.reshape(b * t, d)

    proj = norm_matmul(x2, norm1_g, w_in)
    pos = jnp.tile(past + jnp.arange(t, dtype=I32), b)
    q_bf, iq_bf, k_f32, k_bf, v_bf, ik2_bf, small = attn_prep(proj, pos)
    proj3 = proj.reshape(b, t, IN_COLS)
    small = small.reshape(b, t, LANES)
    new_k = k_f32.reshape(b, t, N_KV_HEADS, HEAD_DIM)
    new_v = proj3[:, :, COL_AV:COL_AV + KV_WIDTH].reshape(b, t, N_KV_HEADS, HEAD_DIM)
    new_ik = small[:, :, SMALL_IK:SMALL_IK + IDX_HEAD_DIM]
    i_w = small[:, :, SMALL_IW:SMALL_IW + N_IDX_HEADS]

    tq, tk = attn_tiles
    lp = -(-n_keys // tk) * tk

    def with_past(old_bf, new_bf):
        allk = jnp.concatenate([old_bf, new_bf.reshape(b, t, -1)], axis=1)
        return jnp.pad(allk, ((0, 0), (0, lp - n_keys), (0, 0)))

    pik = past_ik.astype(BF16)
    zpad = jnp.zeros_like(pik)
    past_ik2 = jnp.concatenate([pik, zpad, zpad, pik], axis=-1)
    attn = dsa_attention(
        iq_bf.reshape(b, t, IDX_WIDTH), i_w, with_past(past_ik2, ik2_bf), q_bf.reshape(b, t, ATTN_WIDTH),
        with_past(past_k.reshape(b, past, KV_WIDTH).astype(BF16), k_bf),
        with_past(past_v.reshape(b, past, KV_WIDTH).astype(BF16), v_bf),
        n_keys=n_keys, past=past, tq=tq, tk=tk)

    new_dn_buf = jnp.concatenate([dn_buf, proj3[:, :, COL_DQ:COL_DQ + 3 * DN_WIDTH]], axis=1)[:, t:] if t < DN_CONV - 1 \
        else proj3[:, t - (DN_CONV - 1):, COL_DQ:COL_DQ + 3 * DN_WIDTH]
    beta = jax.nn.sigmoid(small[:, :, SMALL_DB:SMALL_DB + DN_HEADS])
    g = -jnp.exp(dn_a_log) * jax.nn.softplus(small[:, :, SMALL_DA:SMALL_DA + DN_HEADS] + dn_dt_bias)
    chunk = CHUNK if t % CHUNK == 0 else t
    gc = jnp.cumsum(g.reshape(b, t // chunk, chunk, DN_HEADS), axis=2).reshape(b, t, DN_HEADS)
    o_dn, new_state = delta_rule(proj3, dn_buf, dn_conv_w, gc, beta, dn_norm_g, dn_state, chunk)

    x2 = matmul_res([attn.reshape(b * t, ATTN_WIDTH), o_dn.reshape(b * t, DN_WIDTH)], w_out, x2)

    f2 = ffn_w_up.shape[1]
    xt = jnp.swapaxes(x2.reshape(b, t, d), 0, 1).reshape(t * b, d)
    halo = max(SUBLANES, (FFN_CONV - 1) * b)
    hist = jnp.swapaxes(ffn_buf, 0, 1).reshape((FFN_CONV - 1) * b, f2)
    hist = ffn_interleave(jnp.pad(hist, ((halo - hist.shape[0], 0), (0, 0))))
    act, tail = ffn_up(xt, norm2_g, ffn_w_up, ffn_conv_w, hist, step=b)
    tail = ffn_deinterleave(tail[halo - (FFN_CONV - 1) * b:])
    new_ffn_buf = jnp.swapaxes(tail.reshape(FFN_CONV - 1, b, f2), 0, 1)
    act = jnp.swapaxes(act.reshape(t, b, -1), 0, 1).reshape(b * t, -1)
    x2 = matmul_res([act], ffn_w_down, x2)
    return x2.reshape(b, t, d), new_k, new_v, new_ik, new_dn_buf, new_state, new_ffn_buf


def kernel(x_prompt, x_sample, cache_k, cache_v, cache_kidx, state_dn_conv, state_dn, state_ffn_conv,
           norm1_g, w_in, dn_conv_w, dn_a_log, dn_dt_bias, dn_norm_g, w_out, norm2_g,
           ffn_w_up, ffn_conv_w, ffn_w_down, final_g):
    depth = w_in.shape[0]
    bp, tp, d = x_prompt.shape
    bs, ts, _ = x_sample.shape
    f2 = ffn_w_up.shape[2]
    zk = jnp.zeros((bp, 0, N_KV_HEADS, HEAD_DIM), F32)
    zik = jnp.zeros((bp, 0, IDX_HEAD_DIM), F32)
    z_dn_buf = jnp.zeros((bp, DN_CONV - 1, 3 * DN_WIDTH), F32)
    z_dn_state = jnp.zeros((bp, DN_HEADS, DN_HEAD_DIM, DN_HEAD_DIM), F32)
    z_ffn_buf = jnp.zeros((bp, FFN_CONV - 1, f2), F32)

    tq_p = _pick(tp, (256, 128, 64, 32, 16))
    tk_p = _pick(tp, (512, 256, 128))
    n_keys_s = cache_k.shape[2] + ts
    tk_s = -(-n_keys_s // LANES) * LANES
    perm = _in_proj_perm()

    xp, xs = x_prompt, x_sample
    st_p, st_s = [], []
    for l in range(depth):
        lw = (norm1_g[l], w_in[l][:, perm].astype(BF16), dn_conv_w[l], dn_a_log[l], dn_dt_bias[l], dn_norm_g[l],
              w_out[l].astype(BF16), norm2_g[l], ffn_interleave(ffn_w_up[l]).astype(BF16),
              ffn_interleave(ffn_conv_w[l]), ffn_w_down[l].astype(BF16))
        xp, *sp = _layer(xp, zk, zk, zik, z_dn_buf, z_dn_state, z_ffn_buf, lw, attn_tiles=(tq_p, tk_p))
        xs, *ss = _layer(xs, cache_k[l], cache_v[l], cache_kidx[l], state_dn_conv[l], state_dn[l],
                         state_ffn_conv[l], lw, attn_tiles=(ts, tk_s))
        st_p.append(sp)
        st_s.append(ss)

    y_prompt = rms_norm_rows(xp.reshape(bp * tp, d), final_g).reshape(bp, tp, d)
    y_sample = rms_norm_rows(xs.reshape(bs * ts, d), final_g).reshape(bs, ts, d)

    def stacked(states, i):
        return jnp.stack([s[i] for s in states], axis=0)

    return (y_prompt, y_sample,
            *[stacked(st_p, i) for i in range(6)],
            *[stacked(st_s, i) for i in range(6)])
```

```python
import functools

import numpy as np
import jax
import jax.numpy as jnp
from jax import lax
from jax.experimental import pallas as pl
from jax.experimental.pallas import tpu as pltpu

F32 = jnp.float32
BF16 = jnp.bfloat16
I32 = jnp.int32

CHUNK = 64
CHUNK_SHIFT = CHUNK.bit_length() - 1
HEAD_DIM = 128
N_HEADS = 16
N_KV_HEADS = 4
KV_GROUP = N_HEADS // N_KV_HEADS
ATTN_WIDTH = N_HEADS * HEAD_DIM
KV_WIDTH = N_KV_HEADS * HEAD_DIM
N_IDX_HEADS = 32
IDX_HEAD_DIM = 64
IDX_WIDTH = N_IDX_HEADS * IDX_HEAD_DIM
IDX_SCALE = IDX_WIDTH ** -0.5
IDX_TOPK_MAX = 256
DN_HEADS = 16
DN_HEAD_DIM = 128
DN_WIDTH = DN_HEADS * DN_HEAD_DIM
DN_CONV = 4
FFN_CONV = 3
ROPE_THETA = 10000.0
EPS = 1e-6
LOG2E = 1.4426950408889634

LANES = 128
SUBLANES = 8
VMEM_LIMIT = 56 * 1024 * 1024

INT_MIN = -(2 ** 31)
INT_MAX = 2 ** 31 - 1
NEG_BIG = -1e30

COL_AQ, COL_IQ, COL_DQ, COL_DK, COL_DV, COL_DZ = (i * ATTN_WIDTH for i in range(6))
COL_AK = 6 * ATTN_WIDTH
COL_AV = COL_AK + KV_WIDTH
COL_SMALL = COL_AV + KV_WIDTH
SMALL_IK, SMALL_IW, SMALL_DB, SMALL_DA = 0, 64, 96, 112
IN_TILE = 768
IN_COLS = -(-(COL_SMALL + LANES) // IN_TILE) * IN_TILE
assert ATTN_WIDTH == IDX_WIDTH == DN_WIDTH


def _regroup_in_proj(w):
    splits = (ATTN_WIDTH, KV_WIDTH, KV_WIDTH, IDX_WIDTH, IDX_HEAD_DIM, N_IDX_HEADS,
              DN_WIDTH, DN_WIDTH, DN_WIDTH, DN_HEADS, DN_HEADS, DN_WIDTH)
    starts = np.concatenate([[0], np.cumsum(splits)])
    seg = {n: (int(starts[i]), int(starts[i + 1])) for i, n in enumerate(
        ("a_q", "a_k", "a_v", "i_q", "i_k", "i_w", "d_q", "d_k", "d_v", "d_b", "d_a", "d_z"))}
    order = ("a_q", "i_q", "d_q", "d_k", "d_v", "d_z", "a_k", "a_v", "i_k", "i_w", "d_b", "d_a")
    parts = [w[:, seg[n][0]:seg[n][1]] for n in order]
    parts.append(jnp.zeros((w.shape[0], IN_COLS - int(starts[-1])), w.dtype))
    return jnp.concatenate(parts, axis=1)


def _cparams(sem):
    return pltpu.CompilerParams(dimension_semantics=sem, vmem_limit_bytes=VMEM_LIMIT)


def _pick(n, prefs):
    for p in prefs:
        if n % p == 0:
            return p
    raise ValueError(f"no tile in {prefs} divides {n}")


ROW_TILES = (512, 256, 128, 64, 32, 16, 8)


def _norm_matmul_kernel(x_ref, g_ref, w_ref, o_ref, h_ref):
    @pl.when(pl.program_id(1) == 0)
    def _():
        x = x_ref[...]
        ms = jnp.mean(x * x, axis=-1, keepdims=True)
        h_ref[...] = (x * lax.rsqrt(ms + EPS) * g_ref[...]).astype(BF16)

    o_ref[...] = jnp.dot(h_ref[...], w_ref[...], preferred_element_type=F32)


def norm_matmul(x, g, w):
    n, d = x.shape
    cols = w.shape[1]
    tm = _pick(n, ROW_TILES)
    tn = _pick(cols, (IN_TILE, 512, 256, 128))
    return pl.pallas_call(
        _norm_matmul_kernel,
        grid=(n // tm, cols // tn),
        in_specs=[pl.BlockSpec((tm, d), lambda i, j: (i, 0)),
                  pl.BlockSpec((1, d), lambda i, j: (0, 0)),
                  pl.BlockSpec((d, tn), lambda i, j: (0, j))],
        out_specs=pl.BlockSpec((tm, tn), lambda i, j: (i, j)),
        out_shape=jax.ShapeDtypeStruct((n, cols), F32),
        scratch_shapes=[pltpu.VMEM((tm, d), BF16)],
        compiler_params=_cparams(("parallel", "arbitrary")),
        name="norm_matmul",
    )(x, g.reshape(1, d), w)


def _matmul_res_kernel(*refs, n_lhs):
    a_refs, w_refs, x_ref, o_ref = refs[:n_lhs], refs[n_lhs:2 * n_lhs], refs[2 * n_lhs], refs[2 * n_lhs + 1]
    acc = x_ref[...]
    for a_ref, w_ref in zip(a_refs, w_refs):
        acc = acc + jnp.dot(a_ref[...], w_ref[...], preferred_element_type=F32)
    o_ref[...] = acc


def matmul_res(lhs, w, x):
    n, k = lhs[0].shape
    cols = w.shape[1]
    tm = _pick(n, ROW_TILES)
    tn = _pick(cols, (256,) if k > 8192 else (1024, 512, 256, 128))
    kern = functools.partial(_matmul_res_kernel, n_lhs=len(lhs))
    return pl.pallas_call(
        kern,
        grid=(n // tm, cols // tn),
        in_specs=([pl.BlockSpec((tm, k), lambda i, j: (i, 0)) for _ in lhs]
                  + [pl.BlockSpec((k, tn), lambda i, j, r=r: (r, j)) for r in range(len(lhs))]
                  + [pl.BlockSpec((tm, tn), lambda i, j: (i, j))]),
        out_specs=pl.BlockSpec((tm, tn), lambda i, j: (i, j)),
        out_shape=jax.ShapeDtypeStruct((n, cols), F32),
        compiler_params=_cparams(("parallel", "arbitrary")),
        name="matmul_res",
    )(*lhs, *([w] * len(lhs)), x)


def _rms_norm_kernel(x_ref, g_ref, o_ref):
    x = x_ref[...]
    ms = jnp.mean(x * x, axis=-1, keepdims=True)
    o_ref[...] = x * lax.rsqrt(ms + EPS) * g_ref[...]


def rms_norm_rows(x, g):
    n, d = x.shape
    tm = _pick(n, ROW_TILES)
    return pl.pallas_call(
        _rms_norm_kernel,
        grid=(n // tm,),
        in_specs=[pl.BlockSpec((tm, d), lambda i: (i, 0)), pl.BlockSpec((1, d), lambda i: (0, 0))],
        out_specs=pl.BlockSpec((tm, d), lambda i: (i, 0)),
        out_shape=jax.ShapeDtypeStruct((n, d), F32),
        compiler_params=_cparams(("parallel",)),
        name="rms_norm",
    )(x, g.reshape(1, d))


FFN_TILE = 256


def ffn_interleave(a):
    f = a.shape[-1] // 2
    lead = a.shape[:-1]
    return jnp.swapaxes(a.reshape(*lead, 2, f // FFN_TILE, FFN_TILE), -3, -2).reshape(*lead, 2 * f)


def ffn_deinterleave(a):
    f = a.shape[-1] // 2
    lead = a.shape[:-1]
    return jnp.swapaxes(a.reshape(*lead, f // FFN_TILE, 2, FFN_TILE), -3, -2).reshape(*lead, 2 * f)


def _ffn_up_kernel(x_ref, g_ref, w_ref, c_ref, hist_ref, a_ref, st_ref, h_ref, e_ref, cy_ref, *, tm, tn, step, halo):
    i = pl.program_id(0)
    j = pl.program_id(1)

    @pl.when(j == 0)
    def _():
        x = x_ref[...]
        ms = jnp.mean(x * x, axis=-1, keepdims=True)
        h_ref[...] = (x * lax.rsqrt(ms + EPS) * g_ref[...]).astype(BF16)

    up = jnp.dot(h_ref[...], w_ref[...], preferred_element_type=F32)
    e_ref[halo:halo + tm, :] = up

    @pl.when(i == 0)
    def _():
        e_ref[0:halo, :] = hist_ref[...]

    @pl.when(i > 0)
    def _():
        e_ref[0:halo, :] = cy_ref[j]

    c = c_ref[...]
    y = (e_ref[halo - 2 * step:halo - 2 * step + tm, :] * c[0:1, :]
         + e_ref[halo - step:halo - step + tm, :] * c[1:2, :]
         + up * c[2:3, :])
    tail = e_ref[tm:tm + halo, :]
    cy_ref[j] = tail
    st_ref[...] = tail
    yg = y[:, :tn]
    a_ref[...] = (yg * jax.nn.sigmoid(yg) * y[:, tn:]).astype(BF16)


def ffn_up(x, g, w_up, conv_w, hist, step):
    n, d = x.shape
    f2 = w_up.shape[1]
    f = f2 // 2
    halo = hist.shape[0]
    tm = _pick(n, ROW_TILES)
    tn = FFN_TILE
    nj = f // tn
    kern = functools.partial(_ffn_up_kernel, tm=tm, tn=tn, step=step, halo=halo)
    act, tail = pl.pallas_call(
        kern,
        grid=(n // tm, nj),
        in_specs=[pl.BlockSpec((tm, d), lambda i, j: (i, 0)),
                  pl.BlockSpec((1, d), lambda i, j: (0, 0)),
                  pl.BlockSpec((d, 2 * tn), lambda i, j: (0, j)),
                  pl.BlockSpec((FFN_CONV, 2 * tn), lambda i, j: (0, j)),
                  pl.BlockSpec((halo, 2 * tn), lambda i, j: (0, j))],
        out_specs=[pl.BlockSpec((tm, tn), lambda i, j: (i, j)),
                   pl.BlockSpec((None, halo, 2 * tn), lambda i, j: (i, 0, j))],
        out_shape=[jax.ShapeDtypeStruct((n, f), BF16),
                   jax.ShapeDtypeStruct((n // tm, halo, f2), F32)],
        scratch_shapes=[pltpu.VMEM((tm, d), BF16),
                        pltpu.VMEM((tm + halo, 2 * tn), F32),
                        pltpu.VMEM((nj, halo, 2 * tn), F32)],
        compiler_params=_cparams(("arbitrary", "arbitrary")),
        name="ffn_up",
    )(x, g.reshape(1, d), w_up, conv_w, hist)
    return act, tail[-1]


def _attn_prep_kernel(aq_ref, iq_ref, ak_ref, av_ref, sm_ref, cosa_ref, sina_ref, cosi_ref, sini_ref,
                      q_ref, iqo_ref, kf_ref, kb_ref, vb_ref, ik2_ref, smo_ref, *, tm):
    cosa, sina = cosa_ref[...], sina_ref[...]
    cosi, sini = cosi_ref[...], sini_ref[...]
    lane = lax.broadcasted_iota(I32, (tm, LANES), 1)
    first_half = (lane & (IDX_HEAD_DIM - 1)) < IDX_HEAD_DIM // 2

    def rot_head(x):
        return x * cosa + pltpu.roll(x, HEAD_DIM // 2, 1) * sina

    def rot_idx(x):
        partner = jnp.where(first_half, pltpu.roll(x, LANES - IDX_HEAD_DIM // 2, 1), pltpu.roll(x, IDX_HEAD_DIM // 2, 1))
        return x * cosi + partner * sini

    qscale = (HEAD_DIM ** -0.5) * LOG2E
    for h in range(N_HEADS):
        sl = slice(h * LANES, (h + 1) * LANES)
        q_ref[:, sl] = (rot_head(aq_ref[:, sl]) * qscale).astype(BF16)
        iqo_ref[:, sl] = rot_idx(iq_ref[:, sl]).astype(BF16)
    for n in range(N_KV_HEADS):
        sl = slice(n * LANES, (n + 1) * LANES)
        kr = rot_head(ak_ref[:, sl])
        kf_ref[:, sl] = kr
        kb_ref[:, sl] = kr.astype(BF16)
    vb_ref[...] = av_ref[...].astype(BF16)
    sm = sm_ref[...]
    ik = jnp.where(lane < SMALL_IW, rot_idx(sm), 0.0)
    smo_ref[...] = jnp.where(lane < SMALL_IW, ik, jnp.where(lane < SMALL_DB, sm * IDX_SCALE, sm))
    ik2_ref[:, :LANES] = ik.astype(BF16)
    ik2_ref[:, LANES:] = pltpu.roll(ik, IDX_HEAD_DIM, 1).astype(BF16)


def attn_prep(proj, pos):
    n = proj.shape[0]
    tm = _pick(n, (256, 128, 64, 32, 16, 8))
    posf = pos.astype(F32)[:, None]

    def tables(half):
        inv_freq = jnp.float32(ROPE_THETA) ** (-jnp.arange(half, dtype=F32) / half)
        ang = posf * inv_freq[None, :]
        cos, sin = jnp.cos(ang), jnp.sin(ang)
        rep = LANES // (2 * half)
        return jnp.tile(jnp.concatenate([cos, cos], axis=1), (1, rep)), jnp.tile(jnp.concatenate([-sin, sin], axis=1), (1, rep))

    cosa, sina = tables(HEAD_DIM // 2)
    cosi, sini = tables(IDX_HEAD_DIM // 2)
    wide = lambda c: pl.BlockSpec((tm, ATTN_WIDTH), lambda i, c=c: (i, c // ATTN_WIDTH))
    kvw = lambda c: pl.BlockSpec((tm, KV_WIDTH), lambda i, c=c: (i, c // KV_WIDTH))
    tab = pl.BlockSpec((tm, LANES), lambda i: (i, 0))
    row = lambda w: pl.BlockSpec((tm, w), lambda i: (i, 0))
    return pl.pallas_call(
        functools.partial(_attn_prep_kernel, tm=tm),
        grid=(n // tm,),
        in_specs=[wide(COL_AQ), wide(COL_IQ), kvw(COL_AK), kvw(COL_AV),
                  pl.BlockSpec((tm, LANES), lambda i: (i, COL_SMALL // LANES)), tab, tab, tab, tab],
        out_specs=[row(ATTN_WIDTH), row(IDX_WIDTH), row(KV_WIDTH), row(KV_WIDTH), row(KV_WIDTH), row(2 * LANES), row(LANES)],
        out_shape=[jax.ShapeDtypeStruct((n, ATTN_WIDTH), BF16),
                   jax.ShapeDtypeStruct((n, IDX_WIDTH), BF16),
                   jax.ShapeDtypeStruct((n, KV_WIDTH), F32),
                   jax.ShapeDtypeStruct((n, KV_WIDTH), BF16),
                   jax.ShapeDtypeStruct((n, KV_WIDTH), BF16),
                   jax.ShapeDtypeStruct((n, 2 * LANES), BF16),
                   jax.ShapeDtypeStruct((n, LANES), F32)],
        compiler_params=_cparams(("parallel",)),
        name="attn_prep",
    )(proj, proj, proj, proj, proj, cosa, sina, cosi, sini)


IDX_PAIR_GROUP = 4
N_IDX_PAIRS = N_IDX_HEADS // 2


def _dsa_kernel(tb_ref, tqb_ref, tph_ref, tkb_ref, tnkb_ref, tikb_ref, tkvb_ref,
                iq_ref, iw_ref, ik_ref, q_ref, k_ref, v_ref, o_ref,
                keys_ref, sc_ref, kmax_ref, thr_ref, pcut_ref, wb_ref, iqs_ref, qs_ref, m_ref, acc_ref,
                *, tq, tk, n_keys, past, topk, pos_bits):
    s = pl.program_id(0)
    ph = tph_ref[s]
    kb = tkb_ref[s]
    nkb = tnkb_ref[s]
    q0 = tqb_ref[s] * tq + past
    ncol = tk // LANES
    grows = IDX_PAIR_GROUP * tq

    @pl.when(ph == 0)
    def _scores():
        @pl.when(kb == 0)
        def _():
            w = iw_ref[0]
            for h in range(N_IDX_HEADS):
                wb_ref[h] = jnp.broadcast_to(w[:, h:h + 1], (tq, LANES))
            for p in range(N_IDX_PAIRS):
                iqs_ref[p * tq:(p + 1) * tq, :] = iq_ref[0, :, p * LANES:(p + 1) * LANES]

        ik2 = ik_ref[0]
        ik_even, ik_odd = ik2[:, :LANES], ik2[:, LANES:]
        sc_ref[...] = jnp.zeros((tq, tk), F32)

        def group(gi, carry):
            lhs = iqs_ref[pl.ds(pl.multiple_of(gi * grows, grows), grows), :]
            raws = [lax.dot_general(lhs, kk, (((1,), (1,)), ((), ())), preferred_element_type=F32)
                    for kk in (ik_even, ik_odd)]
            part = None
            for j in range(IDX_PAIR_GROUP):
                for par in range(2):
                    wv = wb_ref[(gi * IDX_PAIR_GROUP + j) * 2 + par]
                    wv = jnp.concatenate([wv] * ncol, axis=1) if ncol > 1 else wv
                    term = wv * jnp.maximum(raws[par][j * tq:(j + 1) * tq, :], 0.0)
                    part = term if part is None else part + term
            sc_ref[...] += part
            return carry

        lax.fori_loop(0, N_IDX_PAIRS // IDX_PAIR_GROUP, group, 0)

        sc = sc_ref[...] + 0.0
        bits = pltpu.bitcast(sc, I32)
        key = bits ^ ((bits >> 31) & INT_MAX)
        kpos = kb * tk + lax.broadcasted_iota(I32, (tq, tk), 1)
        qpos = q0 + lax.broadcasted_iota(I32, (tq, tk), 0)
        adm = jnp.logical_and((kpos >> CHUNK_SHIFT) <= (qpos >> CHUNK_SHIFT), kpos < n_keys)
        key = jnp.where(adm, key, INT_MIN)
        keys_ref[kb] = key
        kmax = key[:, :LANES]
        for c in range(1, ncol):
            kmax = jnp.maximum(kmax, key[:, c * LANES:(c + 1) * LANES])

        @pl.when(kb == 0)
        def _():
            kmax_ref[...] = kmax

        @pl.when(kb > 0)
        def _():
            kmax_ref[...] = jnp.maximum(kmax_ref[...], kmax)

    @pl.when(ph == 1)
    def _attend():
        def count_rows(pred):
            def blk(b, cnt):
                hit = pred(keys_ref[b], b).astype(I32)
                for c in range(ncol):
                    cnt = cnt + hit[:, c * LANES:(c + 1) * LANES]
                return cnt

            cnt = lax.fori_loop(0, nkb, blk, jnp.zeros((tq, LANES), I32))
            return jnp.sum(cnt, axis=1, keepdims=True)

        @pl.when(kb == 0)
        def _select():
            qrow = q0 + lax.broadcasted_iota(I32, (tq, 1), 0)
            n_adm = jnp.minimum(((qrow >> CHUNK_SHIFT) + 1) << CHUNK_SHIFT, n_keys)
            lo0 = jnp.full((tq, 1), INT_MIN + 1, I32)
            hi0 = jnp.max(kmax_ref[...], axis=1, keepdims=True) + 1
            done0 = (n_adm <= topk).astype(I32)

            def bis_cond(c):
                return jnp.logical_and(c[0] < 34, c[5] > 0)

            def bis_body(c):
                it, lo, hi, cnt_lo, done, _ = c
                mid = (lo >> 1) + (hi >> 1) + (lo & hi & 1)
                tot = count_rows(lambda x, b: x >= mid)
                move = jnp.logical_and(tot >= topk, done == 0)
                shrink = jnp.logical_and(tot < topk, done == 0)
                cnt_new = jnp.where(move, tot, cnt_lo)
                done = jnp.where(jnp.logical_or(cnt_new == topk, mid == lo), 1, done)
                return (it + 1, jnp.where(move, mid, lo), jnp.where(shrink, mid, hi), cnt_new, done,
                        jnp.sum(1 - done))

            _, thr, _, cnt_thr, _, _ = lax.while_loop(
                bis_cond, bis_body, (jnp.int32(0), lo0, hi0, n_adm, done0, jnp.sum(1 - done0)))
            thr_ref[...] = thr
            pcut_ref[...] = jnp.full((tq, 1), INT_MAX, I32)

            tied = cnt_thr > topk

            @pl.when(jnp.max(tied.astype(I32)) > 0)
            def _ties():
                room = topk - count_rows(lambda x, b: x > thr)

                def pos_body(it, cut):
                    cand = cut + lax.shift_left(jnp.int32(1), pos_bits - 1 - it)

                    def pred(x, b):
                        kpos = b * tk + lax.broadcasted_iota(I32, (tq, tk), 1)
                        return jnp.logical_and(x == thr, kpos < cand)

                    return jnp.where(count_rows(pred) < room, cand, cut)

                cut = lax.fori_loop(0, pos_bits, pos_body, jnp.zeros((tq, 1), I32))
                pcut_ref[...] = jnp.where(tied, cut, INT_MAX)

            m_ref[...] = jnp.full(m_ref.shape, NEG_BIG, F32)
            acc_ref[...] = jnp.zeros(acc_ref.shape, F32)
            qv = q_ref[0]
            for n in range(N_KV_HEADS):
                for g in range(KV_GROUP):
                    hd = n * KV_GROUP + g
                    qs_ref[n, g * tq:(g + 1) * tq, :] = qv[:, hd * HEAD_DIM:(hd + 1) * HEAD_DIM]

        key = keys_ref[kb]
        thr = thr_ref[...]
        kpos = kb * tk + lax.broadcasted_iota(I32, (tq, tk), 1)
        keep = jnp.logical_or(key > thr, jnp.logical_and(key == thr, kpos <= pcut_ref[...]))
        bias = jnp.where(keep, 0.0, NEG_BIG).astype(F32)
        bias4 = jnp.concatenate([bias] * KV_GROUP, axis=0)
        kv = k_ref[0]
        vv = v_ref[0]
        ones = jnp.ones((tk, HEAD_DIM), BF16)
        ns = range(N_KV_HEADS)
        lgs = [lax.dot_general(qs_ref[n], kv[:, n * HEAD_DIM:(n + 1) * HEAD_DIM], (((1,), (1,)), ((), ())),
                               preferred_element_type=F32) + bias4 for n in ns]
        ps, alphas = [], []
        for n in ns:
            cols = [lgs[n][:, c * LANES:(c + 1) * LANES] for c in range(ncol)]
            mx = cols[0]
            for c in range(1, ncol):
                mx = jnp.maximum(mx, cols[c])
            m_prev = m_ref[n]
            m_new = jnp.maximum(m_prev, jnp.max(mx, axis=1, keepdims=True))
            alphas.append(jnp.exp2(m_prev - m_new))
            ps.append(jnp.concatenate([jnp.exp2(cols[c] - m_new).astype(BF16) for c in range(ncol)], axis=1))
            m_ref[n] = m_new
        for n in ns:
            ve = jnp.concatenate([vv[:, n * HEAD_DIM:(n + 1) * HEAD_DIM], ones], axis=1)
            pv = jnp.dot(ps[n], ve, preferred_element_type=F32)
            acc_ref[n] = jnp.concatenate([alphas[n], alphas[n]], axis=1) * acc_ref[n] + pv

        @pl.when(kb == nkb - 1)
        def _finish():
            for n in range(N_KV_HEADS):
                a = acc_ref[n]
                o = a[:, :HEAD_DIM] / a[:, HEAD_DIM:]
                for g in range(KV_GROUP):
                    hd = n * KV_GROUP + g
                    o_ref[0, :, hd * HEAD_DIM:(hd + 1) * HEAD_DIM] = o[g * tq:(g + 1) * tq, :].astype(BF16)


def _dsa_tables(batch, t, tq, tk, past, n_keys):
    rows = []
    for b in range(batch):
        for qb in range(t // tq):
            last_q = past + qb * tq + tq - 1
            limit = min(n_keys, (last_q // CHUNK + 1) * CHUNK)
            nkb = -(-limit // tk)
            for ph in range(2):
                for kb in range(nkb):
                    ikb = kb if ph == 0 else nkb - 1
                    kvb = 0 if ph == 0 else kb
                    rows.append((b, qb, ph, kb, nkb, ikb, kvb))
    tab = np.asarray(rows, dtype=np.int32).T
    return [jnp.asarray(tab[r]) for r in range(tab.shape[0])]


def dsa_attention(iq, iw, ik2, q, k, v, *, n_keys, past, tq, tk):
    batch, t, _ = iq.shape
    lp = ik2.shape[1]
    assert t % tq == 0 and lp % tk == 0 and tk % LANES == 0
    topk = min(IDX_TOPK_MAX, n_keys // 4)
    tables = _dsa_tables(batch, t, tq, tk, past, n_keys)
    nsteps = int(tables[0].shape[0])
    nkb_max = lp // tk
    kern = functools.partial(_dsa_kernel, tq=tq, tk=tk, n_keys=n_keys, past=past, topk=topk,
                             pos_bits=max(1, (lp - 1).bit_length()))
    qmap = lambda s, tb, tqb, tph, tkb, tnkb, tikb, tkvb: (tb[s], tqb[s], 0)
    grid_spec = pltpu.PrefetchScalarGridSpec(
        num_scalar_prefetch=7,
        grid=(nsteps,),
        in_specs=[
            pl.BlockSpec((1, tq, IDX_WIDTH), qmap),
            pl.BlockSpec((1, tq, N_IDX_HEADS), qmap),
            pl.BlockSpec((1, tk, 2 * LANES), lambda s, tb, tqb, tph, tkb, tnkb, tikb, tkvb: (tb[s], tikb[s], 0)),
            pl.BlockSpec((1, tq, ATTN_WIDTH), qmap),
            pl.BlockSpec((1, tk, KV_WIDTH), lambda s, tb, tqb, tph, tkb, tnkb, tikb, tkvb: (tb[s], tkvb[s], 0)),
            pl.BlockSpec((1, tk, KV_WIDTH), lambda s, tb, tqb, tph, tkb, tnkb, tikb, tkvb: (tb[s], tkvb[s], 0)),
        ],
        out_specs=pl.BlockSpec((1, tq, ATTN_WIDTH), qmap),
        scratch_shapes=[
            pltpu.VMEM((nkb_max, tq, tk), I32),
            pltpu.VMEM((tq, tk), F32),
            pltpu.VMEM((tq, LANES), I32),
            pltpu.VMEM((tq, 1), I32),
            pltpu.VMEM((tq, 1), I32),
            pltpu.VMEM((N_IDX_HEADS, tq, LANES), F32),
            pltpu.VMEM((N_IDX_PAIRS * tq, LANES), BF16),
            pltpu.VMEM((N_KV_HEADS, KV_GROUP * tq, HEAD_DIM), BF16),
            pltpu.VMEM((N_KV_HEADS, KV_GROUP * tq, LANES), F32),
            pltpu.VMEM((N_KV_HEADS, KV_GROUP * tq, 2 * HEAD_DIM), F32),
        ],
    )
    return pl.pallas_call(
        kern,
        grid_spec=grid_spec,
        out_shape=jax.ShapeDtypeStruct((batch, t, ATTN_WIDTH), BF16),
        compiler_params=_cparams(("arbitrary",)),
        name="dsa_attention",
    )(*tables, iq, iw, ik2, q, k, v)


def _mm(a, b):
    return jnp.dot(a.astype(BF16), b.astype(BF16), preferred_element_type=F32)


def _mm_nt(a, b):
    return lax.dot_general(a.astype(BF16), b.astype(BF16), (((1,), (1,)), ((), ())), preferred_element_type=F32)


def _mm_tn(a, b):
    return lax.dot_general(a.astype(BF16), b.astype(BF16), (((0,), (0,)), ((), ())), preferred_element_type=F32)


def _delta_kernel(dq_ref, dk_ref, dv_ref, z_ref, bq_ref, bk_ref, bv_ref, cq_ref, ck_ref, cv_ref,
                  gc_ref, gr_ref, b_ref, gn_ref, s0_ref, o_ref, s_ref, eq_ref, ek_ref, ev_ref,
                  *, chunk, n_double):
    first = pl.program_id(1) == 0

    @pl.when(first)
    def _():
        s_ref[...] = s0_ref[...]

    def conv_silu(x_ref, buf_ref, cw_ref, e_ref):
        @pl.when(first)
        def _():
            e_ref[0:SUBLANES, :] = buf_ref[0]

        e_ref[SUBLANES:SUBLANES + chunk, :] = x_ref[0]
        w = cw_ref[...]
        y = e_ref[SUBLANES:SUBLANES + chunk, :] * w[DN_CONV - 1:DN_CONV, :]
        for i in range(DN_CONV - 1):
            off = SUBLANES - (DN_CONV - 1) + i
            y = y + e_ref[off:off + chunk, :] * w[i:i + 1, :]
        e_ref[0:SUBLANES, :] = e_ref[chunk:chunk + SUBLANES, :]
        return y * jax.nn.sigmoid(y)

    yq = conv_silu(dq_ref, bq_ref, cq_ref, eq_ref)
    yk = conv_silu(dk_ref, bk_ref, ck_ref, ek_ref)
    yv = conv_silu(dv_ref, bv_ref, cv_ref, ev_ref)

    row = lax.broadcasted_iota(I32, (chunk, chunk), 0)
    col = lax.broadcasted_iota(I32, (chunk, chunk), 1)
    eye = (row == col).astype(F32)
    gcs = gc_ref[0]
    grs = gr_ref[0, 0]
    betas = b_ref[0]
    gn = gn_ref[...]
    hs = range(DN_HEADS)
    sl = [slice(h * DN_HEAD_DIM, (h + 1) * DN_HEAD_DIM) for h in hs]

    def l2n(x):
        return x * lax.rsqrt(jnp.sum(x * x, axis=-1, keepdims=True) + EPS)

    kh = [l2n(yk[:, sl[h]]) for h in hs]
    qh = [l2n(yq[:, sl[h]]) * (DN_HEAD_DIM ** -0.5) for h in hs]
    gc = [gcs[:, h:h + 1] for h in hs]
    beta = [betas[:, h:h + 1] for h in hs]
    decay = [jnp.exp(jnp.where(row >= col, gc[h] - grs[h:h + 1, :], -jnp.inf)) for h in hs]
    kk = [_mm_nt(kh[h], kh[h]) for h in hs]
    qk = [_mm_nt(qh[h], kh[h]) * decay[h] for h in hs]
    x = [-jnp.where(row > col, beta[h] * kk[h] * decay[h], 0.0) for h in hs]
    parts = [x]
    for _ in range(n_double):
        x = [_mm(x[h], x[h]) for h in hs]
        parts.append(x)
    while len(parts) > 1:
        nxt = [[parts[i][h] + parts[i + 1][h] + _mm(parts[i][h], parts[i + 1][h]) for h in hs]
               for i in range(0, len(parts) - 1, 2)]
        if len(parts) % 2:
            nxt.append(parts[-1])
        parts = nxt
    t = [eye + parts[0][h] for h in hs]
    eg = [jnp.exp(gc[h]) for h in hs]
    uw = [_mm(t[h], jnp.concatenate([yv[:, sl[h]] * beta[h], kh[h] * (beta[h] * eg[h])], axis=1)) for h in hs]
    s = [s_ref[0, h] for h in hs]
    ws_qs = [_mm(jnp.concatenate([uw[h][:, DN_HEAD_DIM:], qh[h] * eg[h]], axis=0), s[h]) for h in hs]
    v_new = [uw[h][:, :DN_HEAD_DIM] - ws_qs[h][:chunk] for h in hs]
    gl = [gc[h][chunk - 1:chunk, :] for h in hs]
    o = [ws_qs[h][chunk:] + _mm(qk[h], v_new[h]) for h in hs]
    upd = [_mm_tn(kh[h] * jnp.exp(gl[h] - gc[h]), v_new[h]) for h in hs]
    for h in hs:
        s_ref[0, h] = s[h] * jnp.exp(gl[h]) + upd[h]
        ms = jnp.mean(o[h] * o[h], axis=-1, keepdims=True)
        on = o[h] * lax.rsqrt(ms + EPS) * gn
        z = z_ref[0, :, sl[h]]
        o_ref[0, :, sl[h]] = (on * (z * jax.nn.sigmoid(z))).astype(BF16)


def delta_rule(proj, dn_buf, conv_w, gc, beta, gnorm, state, chunk):
    batch, t, _ = proj.shape
    n = t // chunk
    n_double = max(0, (chunk - 1).bit_length() - 1)
    gr = jnp.swapaxes(gc.reshape(batch, n, chunk, DN_HEADS), 2, 3)
    buf8 = jnp.pad(dn_buf, ((0, 0), (SUBLANES - (DN_CONV - 1), 0), (0, 0)))
    kern = functools.partial(_delta_kernel, chunk=chunk, n_double=n_double)
    pcol = lambda c: pl.BlockSpec((1, chunk, DN_WIDTH), lambda b, i, c=c: (b, i, c // DN_WIDTH))
    bcol = lambda j: pl.BlockSpec((1, SUBLANES, DN_WIDTH), lambda b, i, j=j: (b, 0, j))
    wcol = lambda j: pl.BlockSpec((DN_CONV, DN_WIDTH), lambda b, i, j=j: (0, j))
    wide = pl.BlockSpec((1, chunk, DN_WIDTH), lambda b, i: (b, i, 0))
    narrow = pl.BlockSpec((1, chunk, DN_HEADS), lambda b, i: (b, i, 0))
    st = pl.BlockSpec((1, DN_HEADS, DN_HEAD_DIM, DN_HEAD_DIM), lambda b, i: (b, 0, 0, 0))
    return pl.pallas_call(
        kern,
        grid=(batch, n),
        in_specs=[pcol(COL_DQ), pcol(COL_DK), pcol(COL_DV), pcol(COL_DZ),
                  bcol(0), bcol(1), bcol(2), wcol(0), wcol(1), wcol(2),
                  narrow,
                  pl.BlockSpec((1, 1, DN_HEADS, chunk), lambda b, i: (b, i, 0, 0)),
                  narrow,
                  pl.BlockSpec((1, DN_HEAD_DIM), lambda b, i: (0, 0)),
                  st],
        out_specs=[wide, st],
        out_shape=[jax.ShapeDtypeStruct((batch, t, DN_WIDTH), BF16),
                   jax.ShapeDtypeStruct(state.shape, F32)],
        scratch_shapes=[pltpu.VMEM((chunk + SUBLANES, DN_WIDTH), F32)] * 3,
        compiler_params=_cparams(("parallel", "arbitrary")),
        name="delta_rule",
    )(proj, proj, proj, proj, buf8, buf8, buf8, conv_w, conv_w, conv_w,
      gc, gr, beta, gnorm.reshape(1, DN_HEAD_DIM), state)


def _layer(x, past_k, past_v, past_ik, dn_buf, dn_state, ffn_buf, lw, *, attn_tiles):
    (norm1_g, w_in, dn_conv_w, dn_a_log, dn_dt_bias, dn_norm_g, w_out, norm2_g, ffn_w_up, ffn_conv_w,
     ffn_w_down) = lw
    b, t, d = x.shape
    past = past_k.shape[1]
    n_keys = past + t
    x2 = x.reshape(b * t, d)

    proj = norm_matmul(x2, norm1_g, w_in)
    pos = jnp.tile(past + jnp.arange(t, dtype=I32), b)
    q_bf, iq_bf, k_f32, k_bf, v_bf, ik2_bf, small = attn_prep(proj, pos)
    proj3 = proj.reshape(b, t, IN_COLS)
    small = small.reshape(b, t, LANES)
    new_k = k_f32.reshape(b, t, N_KV_HEADS, HEAD_DIM)
    new_v = proj3[:, :, COL_AV:COL_AV + KV_WIDTH].reshape(b, t, N_KV_HEADS, HEAD_DIM)
    new_ik = small[:, :, SMALL_IK:SMALL_IK + IDX_HEAD_DIM]
    i_w = small[:, :, SMALL_IW:SMALL_IW + N_IDX_HEADS]

    tq, tk = attn_tiles
    lp = -(-n_keys // tk) * tk

    def with_past(old_bf, new_bf):
        allk = jnp.concatenate([old_bf, new_bf.reshape(b, t, -1)], axis=1)
        return jnp.pad(allk, ((0, 0), (0, lp - n_keys), (0, 0)))

    pik = past_ik.astype(BF16)
    zpad = jnp.zeros_like(pik)
    past_ik2 = jnp.concatenate([pik, zpad, zpad, pik], axis=-1)
    attn = dsa_attention(
        iq_bf.reshape(b, t, IDX_WIDTH), i_w, with_past(past_ik2, ik2_bf), q_bf.reshape(b, t, ATTN_WIDTH),
        with_past(past_k.reshape(b, past, KV_WIDTH).astype(BF16), k_bf),
        with_past(past_v.reshape(b, past, KV_WIDTH).astype(BF16), v_bf),
        n_keys=n_keys, past=past, tq=tq, tk=tk)

    new_dn_buf = jnp.concatenate([dn_buf, proj3[:, :, COL_DQ:COL_DQ + 3 * DN_WIDTH]], axis=1)[:, t:] if t < DN_CONV - 1 \
        else proj3[:, t - (DN_CONV - 1):, COL_DQ:COL_DQ + 3 * DN_WIDTH]
    beta = jax.nn.sigmoid(small[:, :, SMALL_DB:SMALL_DB + DN_HEADS])
    g = -jnp.exp(dn_a_log) * jax.nn.softplus(small[:, :, SMALL_DA:SMALL_DA + DN_HEADS] + dn_dt_bias)
    chunk = CHUNK if t % CHUNK == 0 else t
    gc = jnp.cumsum(g.reshape(b, t // chunk, chunk, DN_HEADS), axis=2).reshape(b, t, DN_HEADS)
    o_dn, new_state = delta_rule(proj3, dn_buf, dn_conv_w, gc, beta, dn_norm_g, dn_state, chunk)

    x2 = matmul_res([attn.reshape(b * t, ATTN_WIDTH), o_dn.reshape(b * t, DN_WIDTH)], w_out, x2)

    f2 = ffn_w_up.shape[1]
    xt = jnp.swapaxes(x2.reshape(b, t, d), 0, 1).reshape(t * b, d)
    halo = max(SUBLANES, (FFN_CONV - 1) * b)
    hist = jnp.swapaxes(ffn_buf, 0, 1).reshape((FFN_CONV - 1) * b, f2)
    hist = ffn_interleave(jnp.pad(hist, ((halo - hist.shape[0], 0), (0, 0))))
    act, tail = ffn_up(xt, norm2_g, ffn_w_up, ffn_conv_w, hist, step=b)
    tail = ffn_deinterleave(tail[halo - (FFN_CONV - 1) * b:])
    new_ffn_buf = jnp.swapaxes(tail.reshape(FFN_CONV - 1, b, f2), 0, 1)
    act = jnp.swapaxes(act.reshape(t, b, -1), 0, 1).reshape(b * t, -1)
    x2 = matmul_res([act], ffn_w_down, x2)
    return x2.reshape(b, t, d), new_k, new_v, new_ik, new_dn_buf, new_state, new_ffn_buf


def kernel(x_prompt, x_sample, cache_k, cache_v, cache_kidx, state_dn_conv, state_dn, state_ffn_conv,
           norm1_g, w_in, dn_conv_w, dn_a_log, dn_dt_bias, dn_norm_g, w_out, norm2_g,
           ffn_w_up, ffn_conv_w, ffn_w_down, final_g):
    depth = w_in.shape[0]
    bp, tp, d = x_prompt.shape
    bs, ts, _ = x_sample.shape
    f2 = ffn_w_up.shape[2]
    zk = jnp.zeros((bp, 0, N_KV_HEADS, HEAD_DIM), F32)
    zik = jnp.zeros((bp, 0, IDX_HEAD_DIM), F32)
    z_dn_buf = jnp.zeros((bp, DN_CONV - 1, 3 * DN_WIDTH), F32)
    z_dn_state = jnp.zeros((bp, DN_HEADS, DN_HEAD_DIM, DN_HEAD_DIM), F32)
    z_ffn_buf = jnp.zeros((bp, FFN_CONV - 1, f2), F32)

    tq_p = _pick(tp, (256, 128, 64, 32, 16))
    tk_p = _pick(tp, (512, 256, 128))
    n_keys_s = cache_k.shape[2] + ts
    tk_s = -(-n_keys_s // LANES) * LANES

    xp, xs = x_prompt, x_sample
    st_p, st_s = [], []
    for l in range(depth):
        lw = (norm1_g[l], _regroup_in_proj(w_in[l].astype(BF16)), dn_conv_w[l], dn_a_log[l], dn_dt_bias[l],
              dn_norm_g[l], w_out[l].astype(BF16), norm2_g[l], ffn_interleave(ffn_w_up[l].astype(BF16)),
              ffn_interleave(ffn_conv_w[l]), ffn_w_down[l].astype(BF16))
        xp, *sp = _layer(xp, zk, zk, zik, z_dn_buf, z_dn_state, z_ffn_buf, lw, attn_tiles=(tq_p, tk_p))
        xs, *ss = _layer(xs, cache_k[l], cache_v[l], cache_kidx[l], state_dn_conv[l], state_dn[l],
                         state_ffn_conv[l], lw, attn_tiles=(ts, tk_s))
        st_p.append(sp)
        st_s.append(ss)

    y_prompt = rms_norm_rows(xp.reshape(bp * tp, d), final_g).reshape(bp, tp, d)
    y_sample = rms_norm_rows(xs.reshape(bs * ts, d), final_g).reshape(bs, ts, d)

    def stacked(states, i):
        return jnp.stack([s[i] for s in states], axis=0)

    return (y_prompt, y_sample,
            *[stacked(st_p, i) for i in range(6)],
            *[stacked(st_s, i) for i in range(6)])
```

```python
import functools

import numpy as np
import jax
import jax.numpy as jnp
from jax import lax
from jax.experimental import pallas as pl
from jax.experimental.pallas import tpu as pltpu

F32 = jnp.float32
BF16 = jnp.bfloat16
I32 = jnp.int32

CHUNK = 64
CHUNK_SHIFT = CHUNK.bit_length() - 1
HEAD_DIM = 128
N_HEADS = 16
N_KV_HEADS = 4
KV_GROUP = N_HEADS // N_KV_HEADS
ATTN_WIDTH = N_HEADS * HEAD_DIM
KV_WIDTH = N_KV_HEADS * HEAD_DIM
N_IDX_HEADS = 32
IDX_HEAD_DIM = 64
IDX_WIDTH = N_IDX_HEADS * IDX_HEAD_DIM
IDX_SCALE = IDX_WIDTH ** -0.5
IDX_TOPK_MAX = 256
DN_HEADS = 16
DN_HEAD_DIM = 128
DN_WIDTH = DN_HEADS * DN_HEAD_DIM
DN_CONV = 4
FFN_CONV = 3
ROPE_THETA = 10000.0
EPS = 1e-6
LOG2E = 1.4426950408889634

LANES = 128
SUBLANES = 8
VMEM_LIMIT = 56 * 1024 * 1024

INT_MIN = -(2 ** 31)
INT_MAX = 2 ** 31 - 1
NEG_BIG = -1e30

COL_AQ, COL_IQ, COL_DQ, COL_DK, COL_DV, COL_DZ = (i * ATTN_WIDTH for i in range(6))
COL_AK = 6 * ATTN_WIDTH
COL_AV = COL_AK + KV_WIDTH
COL_SMALL = COL_AV + KV_WIDTH
SMALL_IK, SMALL_IW, SMALL_DB, SMALL_DA = 0, 64, 96, 112
IN_TILE = 768
IN_COLS = -(-(COL_SMALL + LANES) // IN_TILE) * IN_TILE
assert ATTN_WIDTH == IDX_WIDTH == DN_WIDTH


def _regroup_in_proj(w):
    splits = (ATTN_WIDTH, KV_WIDTH, KV_WIDTH, IDX_WIDTH, IDX_HEAD_DIM, N_IDX_HEADS,
              DN_WIDTH, DN_WIDTH, DN_WIDTH, DN_HEADS, DN_HEADS, DN_WIDTH)
    starts = np.concatenate([[0], np.cumsum(splits)])
    seg = {n: (int(starts[i]), int(starts[i + 1])) for i, n in enumerate(
        ("a_q", "a_k", "a_v", "i_q", "i_k", "i_w", "d_q", "d_k", "d_v", "d_b", "d_a", "d_z"))}
    order = ("a_q", "i_q", "d_q", "d_k", "d_v", "d_z", "a_k", "a_v", "i_k", "i_w", "d_b", "d_a")
    parts = [w[:, seg[n][0]:seg[n][1]].astype(BF16) for n in order]
    parts.append(jnp.zeros((w.shape[0], IN_COLS - int(starts[-1])), BF16))
    return jnp.concatenate(parts, axis=1)


def _cparams(sem):
    return pltpu.CompilerParams(dimension_semantics=sem, vmem_limit_bytes=VMEM_LIMIT)


def _pick(n, prefs):
    for p in prefs:
        if n % p == 0:
            return p
    raise ValueError(f"no tile in {prefs} divides {n}")


ROW_TILES = (512, 256, 128, 64, 32, 16, 8)


def _norm_matmul_kernel(x_ref, g_ref, w_ref, o_ref, h_ref):
    @pl.when(pl.program_id(1) == 0)
    def _():
        x = x_ref[...]
        ms = jnp.mean(x * x, axis=-1, keepdims=True)
        h_ref[...] = (x * lax.rsqrt(ms + EPS) * g_ref[...]).astype(BF16)

    o_ref[...] = jnp.dot(h_ref[...], w_ref[...], preferred_element_type=F32)


def norm_matmul(x, g, w):
    n, d = x.shape
    cols = w.shape[1]
    tm = _pick(n, ROW_TILES)
    tn = _pick(cols, (IN_TILE, 512, 256, 128))
    return pl.pallas_call(
        _norm_matmul_kernel,
        grid=(n // tm, cols // tn),
        in_specs=[pl.BlockSpec((tm, d), lambda i, j: (i, 0)),
                  pl.BlockSpec((1, d), lambda i, j: (0, 0)),
                  pl.BlockSpec((d, tn), lambda i, j: (0, j))],
        out_specs=pl.BlockSpec((tm, tn), lambda i, j: (i, j)),
        out_shape=jax.ShapeDtypeStruct((n, cols), F32),
        scratch_shapes=[pltpu.VMEM((tm, d), BF16)],
        compiler_params=_cparams(("parallel", "arbitrary")),
        name="norm_matmul",
    )(x, g.reshape(1, d), w)


def _matmul_res_kernel(*refs, n_lhs):
    a_refs, w_refs, x_ref, o_ref = refs[:n_lhs], refs[n_lhs:2 * n_lhs], refs[2 * n_lhs], refs[2 * n_lhs + 1]
    acc = x_ref[...]
    for a_ref, w_ref in zip(a_refs, w_refs):
        acc = acc + jnp.dot(a_ref[...], w_ref[...], preferred_element_type=F32)
    o_ref[...] = acc


def matmul_res(lhs, w, x):
    n, k = lhs[0].shape
    cols = w.shape[1]
    tm = _pick(n, ROW_TILES)
    tn = _pick(cols, (256,) if k > 8192 else (1024, 512, 256, 128))
    kern = functools.partial(_matmul_res_kernel, n_lhs=len(lhs))
    return pl.pallas_call(
        kern,
        grid=(n // tm, cols // tn),
        in_specs=([pl.BlockSpec((tm, k), lambda i, j: (i, 0)) for _ in lhs]
                  + [pl.BlockSpec((k, tn), lambda i, j, r=r: (r, j)) for r in range(len(lhs))]
                  + [pl.BlockSpec((tm, tn), lambda i, j: (i, j))]),
        out_specs=pl.BlockSpec((tm, tn), lambda i, j: (i, j)),
        out_shape=jax.ShapeDtypeStruct((n, cols), F32),
        compiler_params=_cparams(("parallel", "arbitrary")),
        name="matmul_res",
    )(*lhs, *([w] * len(lhs)), x)


def _rms_norm_kernel(x_ref, g_ref, o_ref):
    x = x_ref[...]
    ms = jnp.mean(x * x, axis=-1, keepdims=True)
    o_ref[...] = x * lax.rsqrt(ms + EPS) * g_ref[...]


def rms_norm_rows(x, g):
    n, d = x.shape
    tm = _pick(n, ROW_TILES)
    return pl.pallas_call(
        _rms_norm_kernel,
        grid=(n // tm,),
        in_specs=[pl.BlockSpec((tm, d), lambda i: (i, 0)), pl.BlockSpec((1, d), lambda i: (0, 0))],
        out_specs=pl.BlockSpec((tm, d), lambda i: (i, 0)),
        out_shape=jax.ShapeDtypeStruct((n, d), F32),
        compiler_params=_cparams(("parallel",)),
        name="rms_norm",
    )(x, g.reshape(1, d))


FFN_TILE = 256


def ffn_interleave(a):
    f = a.shape[-1] // 2
    lead = a.shape[:-1]
    return jnp.swapaxes(a.reshape(*lead, 2, f // FFN_TILE, FFN_TILE), -3, -2).reshape(*lead, 2 * f)


def ffn_deinterleave(a):
    f = a.shape[-1] // 2
    lead = a.shape[:-1]
    return jnp.swapaxes(a.reshape(*lead, f // FFN_TILE, 2, FFN_TILE), -3, -2).reshape(*lead, 2 * f)


def _ffn_up_kernel(x_ref, g_ref, wg_ref, wv_ref, c_ref, hist_ref, a_ref, st_ref, h_ref, e_ref, cy_ref,
                   *, tm, tn, step, halo):
    i = pl.program_id(0)
    j = pl.program_id(1)

    @pl.when(j == 0)
    def _():
        x = x_ref[...]
        ms = jnp.mean(x * x, axis=-1, keepdims=True)
        h_ref[...] = (x * lax.rsqrt(ms + EPS) * g_ref[...]).astype(BF16)

    h = h_ref[...]
    up = jnp.concatenate([jnp.dot(h, wg_ref[...], preferred_element_type=F32),
                          jnp.dot(h, wv_ref[...], preferred_element_type=F32)], axis=1)
    e_ref[halo:halo + tm, :] = up

    @pl.when(i == 0)
    def _():
        e_ref[0:halo, :] = hist_ref[...]

    @pl.when(i > 0)
    def _():
        e_ref[0:halo, :] = cy_ref[j]

    c = c_ref[...]
    y = (e_ref[halo - 2 * step:halo - 2 * step + tm, :] * c[0:1, :]
         + e_ref[halo - step:halo - step + tm, :] * c[1:2, :]
         + up * c[2:3, :])
    tail = e_ref[tm:tm + halo, :]
    cy_ref[j] = tail
    st_ref[...] = tail
    yg = y[:, :tn]
    a_ref[...] = (yg * jax.nn.sigmoid(yg) * y[:, tn:]).astype(BF16)


def ffn_up(x, g, w_up, conv_w, hist, step):
    n, d = x.shape
    f2 = w_up.shape[1]
    f = f2 // 2
    halo = hist.shape[0]
    tm = _pick(n, ROW_TILES)
    tn = FFN_TILE
    nj = f // tn
    kern = functools.partial(_ffn_up_kernel, tm=tm, tn=tn, step=step, halo=halo)
    act, tail = pl.pallas_call(
        kern,
        grid=(n // tm, nj),
        in_specs=[pl.BlockSpec((tm, d), lambda i, j: (i, 0)),
                  pl.BlockSpec((1, d), lambda i, j: (0, 0)),
                  pl.BlockSpec((d, tn), lambda i, j: (0, j)),
                  pl.BlockSpec((d, tn), lambda i, j: (0, nj + j)),
                  pl.BlockSpec((FFN_CONV, 2 * tn), lambda i, j: (0, j)),
                  pl.BlockSpec((halo, 2 * tn), lambda i, j: (0, j))],
        out_specs=[pl.BlockSpec((tm, tn), lambda i, j: (i, j)),
                   pl.BlockSpec((None, halo, 2 * tn), lambda i, j: (i, 0, j))],
        out_shape=[jax.ShapeDtypeStruct((n, f), BF16),
                   jax.ShapeDtypeStruct((n // tm, halo, f2), F32)],
        scratch_shapes=[pltpu.VMEM((tm, d), BF16),
                        pltpu.VMEM((tm + halo, 2 * tn), F32),
                        pltpu.VMEM((nj, halo, 2 * tn), F32)],
        compiler_params=_cparams(("arbitrary", "arbitrary")),
        name="ffn_up",
    )(x, g.reshape(1, d), w_up, w_up, conv_w, hist)
    return act, tail[-1]


def _attn_prep_kernel(aq_ref, iq_ref, ak_ref, av_ref, sm_ref, cosa_ref, sina_ref, cosi_ref, sini_ref,
                      q_ref, iqo_ref, kf_ref, kb_ref, vb_ref, ik2_ref, smo_ref, *, tm):
    cosa, sina = cosa_ref[...], sina_ref[...]
    cosi, sini = cosi_ref[...], sini_ref[...]
    lane = lax.broadcasted_iota(I32, (tm, LANES), 1)
    first_half = (lane & (IDX_HEAD_DIM - 1)) < IDX_HEAD_DIM // 2

    def rot_head(x):
        return x * cosa + pltpu.roll(x, HEAD_DIM // 2, 1) * sina

    def rot_idx(x):
        partner = jnp.where(first_half, pltpu.roll(x, LANES - IDX_HEAD_DIM // 2, 1), pltpu.roll(x, IDX_HEAD_DIM // 2, 1))
        return x * cosi + partner * sini

    qscale = (HEAD_DIM ** -0.5) * LOG2E
    for h in range(N_HEADS):
        sl = slice(h * LANES, (h + 1) * LANES)
        q_ref[:, sl] = (rot_head(aq_ref[:, sl]) * qscale).astype(BF16)
        iqo_ref[:, sl] = rot_idx(iq_ref[:, sl]).astype(BF16)
    for n in range(N_KV_HEADS):
        sl = slice(n * LANES, (n + 1) * LANES)
        kr = rot_head(ak_ref[:, sl])
        kf_ref[:, sl] = kr
        kb_ref[:, sl] = kr.astype(BF16)
    vb_ref[...] = av_ref[...].astype(BF16)
    sm = sm_ref[...]
    ik = jnp.where(lane < SMALL_IW, rot_idx(sm), 0.0)
    smo_ref[...] = jnp.where(lane < SMALL_IW, ik, jnp.where(lane < SMALL_DB, sm * IDX_SCALE, sm))
    ik2_ref[:, :LANES] = ik.astype(BF16)
    ik2_ref[:, LANES:] = pltpu.roll(ik, IDX_HEAD_DIM, 1).astype(BF16)


def attn_prep(proj, pos):
    n = proj.shape[0]
    tm = _pick(n, (256, 128, 64, 32, 16, 8))
    posf = pos.astype(F32)[:, None]

    def tables(half):
        inv_freq = jnp.float32(ROPE_THETA) ** (-jnp.arange(half, dtype=F32) / half)
        ang = posf * inv_freq[None, :]
        cos, sin = jnp.cos(ang), jnp.sin(ang)
        rep = LANES // (2 * half)
        return jnp.tile(jnp.concatenate([cos, cos], axis=1), (1, rep)), jnp.tile(jnp.concatenate([-sin, sin], axis=1), (1, rep))

    cosa, sina = tables(HEAD_DIM // 2)
    cosi, sini = tables(IDX_HEAD_DIM // 2)
    wide = lambda c: pl.BlockSpec((tm, ATTN_WIDTH), lambda i, c=c: (i, c // ATTN_WIDTH))
    kvw = lambda c: pl.BlockSpec((tm, KV_WIDTH), lambda i, c=c: (i, c // KV_WIDTH))
    tab = pl.BlockSpec((tm, LANES), lambda i: (i, 0))
    row = lambda w: pl.BlockSpec((tm, w), lambda i: (i, 0))
    return pl.pallas_call(
        functools.partial(_attn_prep_kernel, tm=tm),
        grid=(n // tm,),
        in_specs=[wide(COL_AQ), wide(COL_IQ), kvw(COL_AK), kvw(COL_AV),
                  pl.BlockSpec((tm, LANES), lambda i: (i, COL_SMALL // LANES)), tab, tab, tab, tab],
        out_specs=[row(ATTN_WIDTH), row(IDX_WIDTH), row(KV_WIDTH), row(KV_WIDTH), row(KV_WIDTH), row(2 * LANES), row(LANES)],
        out_shape=[jax.ShapeDtypeStruct((n, ATTN_WIDTH), BF16),
                   jax.ShapeDtypeStruct((n, IDX_WIDTH), BF16),
                   jax.ShapeDtypeStruct((n, KV_WIDTH), F32),
                   jax.ShapeDtypeStruct((n, KV_WIDTH), BF16),
                   jax.ShapeDtypeStruct((n, KV_WIDTH), BF16),
                   jax.ShapeDtypeStruct((n, 2 * LANES), BF16),
                   jax.ShapeDtypeStruct((n, LANES), F32)],
        compiler_params=_cparams(("parallel",)),
        name="attn_prep",
    )(proj, proj, proj, proj, proj, cosa, sina, cosi, sini)


IDX_PAIR_GROUP = 4
N_IDX_PAIRS = N_IDX_HEADS // 2


def _dsa_kernel(tb_ref, tqb_ref, tph_ref, tkb_ref, tnkb_ref, tikb_ref, tkvb_ref,
                iq_ref, iw_ref, ik_ref, q_ref, k_ref, v_ref, o_ref,
                keys_ref, sc_ref, kmax_ref, thr_ref, pcut_ref, wb_ref, iqs_ref, qs_ref, m_ref, acc_ref,
                *, tq, tk, n_keys, past, topk, pos_bits):
    s = pl.program_id(0)
    ph = tph_ref[s]
    kb = tkb_ref[s]
    nkb = tnkb_ref[s]
    q0 = tqb_ref[s] * tq + past
    ncol = tk // LANES
    grows = IDX_PAIR_GROUP * tq

    @pl.when(ph == 0)
    def _scores():
        @pl.when(kb == 0)
        def _():
            w = iw_ref[0]
            for h in range(N_IDX_HEADS):
                wb_ref[h] = jnp.broadcast_to(w[:, h:h + 1], (tq, LANES))
            for p in range(N_IDX_PAIRS):
                iqs_ref[p * tq:(p + 1) * tq, :] = iq_ref[0, :, p * LANES:(p + 1) * LANES]

        ik2 = ik_ref[0]
        ik_even, ik_odd = ik2[:, :LANES], ik2[:, LANES:]
        sc_ref[...] = jnp.zeros((tq, tk), F32)

        def group(gi, carry):
            lhs = iqs_ref[pl.ds(pl.multiple_of(gi * grows, grows), grows), :]
            raws = [lax.dot_general(lhs, kk, (((1,), (1,)), ((), ())), preferred_element_type=F32)
                    for kk in (ik_even, ik_odd)]
            part = None
            for j in range(IDX_PAIR_GROUP):
                for par in range(2):
                    wv = wb_ref[(gi * IDX_PAIR_GROUP + j) * 2 + par]
                    wv = jnp.concatenate([wv] * ncol, axis=1) if ncol > 1 else wv
                    term = wv * jnp.maximum(raws[par][j * tq:(j + 1) * tq, :], 0.0)
                    part = term if part is None else part + term
            sc_ref[...] += part
            return carry

        lax.fori_loop(0, N_IDX_PAIRS // IDX_PAIR_GROUP, group, 0)

        sc = sc_ref[...] + 0.0
        bits = pltpu.bitcast(sc, I32)
        key = bits ^ ((bits >> 31) & INT_MAX)
        kpos = kb * tk + lax.broadcasted_iota(I32, (tq, tk), 1)
        qpos = q0 + lax.broadcasted_iota(I32, (tq, tk), 0)
        adm = jnp.logical_and((kpos >> CHUNK_SHIFT) <= (qpos >> CHUNK_SHIFT), kpos < n_keys)
        key = jnp.where(adm, key, INT_MIN)
        keys_ref[kb] = key
        kmax = key[:, :LANES]
        for c in range(1, ncol):
            kmax = jnp.maximum(kmax, key[:, c * LANES:(c + 1) * LANES])

        @pl.when(kb == 0)
        def _():
            kmax_ref[...] = kmax

        @pl.when(kb > 0)
        def _():
            kmax_ref[...] = jnp.maximum(kmax_ref[...], kmax)

    @pl.when(ph == 1)
    def _attend():
        def count_rows(pred):
            def blk(b, cnt):
                hit = pred(keys_ref[b], b).astype(I32)
                for c in range(ncol):
                    cnt = cnt + hit[:, c * LANES:(c + 1) * LANES]
                return cnt

            cnt = lax.fori_loop(0, nkb, blk, jnp.zeros((tq, LANES), I32))
            return jnp.sum(cnt, axis=1, keepdims=True)

        @pl.when(kb == 0)
        def _select():
            qrow = q0 + lax.broadcasted_iota(I32, (tq, 1), 0)
            n_adm = jnp.minimum(((qrow >> CHUNK_SHIFT) + 1) << CHUNK_SHIFT, n_keys)
            lo0 = jnp.full((tq, 1), INT_MIN + 1, I32)
            hi0 = jnp.max(kmax_ref[...], axis=1, keepdims=True) + 1
            done0 = (n_adm <= topk).astype(I32)

            def bis_cond(c):
                return jnp.logical_and(c[0] < 34, c[5] > 0)

            def bis_body(c):
                it, lo, hi, cnt_lo, done, _ = c
                mid = (lo >> 1) + (hi >> 1) + (lo & hi & 1)
                tot = count_rows(lambda x, b: x >= mid)
                move = jnp.logical_and(tot >= topk, done == 0)
                shrink = jnp.logical_and(tot < topk, done == 0)
                cnt_new = jnp.where(move, tot, cnt_lo)
                done = jnp.where(jnp.logical_or(cnt_new == topk, mid == lo), 1, done)
                return (it + 1, jnp.where(move, mid, lo), jnp.where(shrink, mid, hi), cnt_new, done,
                        jnp.sum(1 - done))

            _, thr, _, cnt_thr, _, _ = lax.while_loop(
                bis_cond, bis_body, (jnp.int32(0), lo0, hi0, n_adm, done0, jnp.sum(1 - done0)))
            thr_ref[...] = thr
            pcut_ref[...] = jnp.full((tq, 1), INT_MAX, I32)

            tied = cnt_thr > topk

            @pl.when(jnp.max(tied.astype(I32)) > 0)
            def _ties():
                room = topk - count_rows(lambda x, b: x > thr)

                def pos_body(it, cut):
                    cand = cut + lax.shift_left(jnp.int32(1), pos_bits - 1 - it)

                    def pred(x, b):
                        kpos = b * tk + lax.broadcasted_iota(I32, (tq, tk), 1)
                        return jnp.logical_and(x == thr, kpos < cand)

                    return jnp.where(count_rows(pred) < room, cand, cut)

                cut = lax.fori_loop(0, pos_bits, pos_body, jnp.zeros((tq, 1), I32))
                pcut_ref[...] = jnp.where(tied, cut, INT_MAX)

            m_ref[...] = jnp.full(m_ref.shape, NEG_BIG, F32)
            acc_ref[...] = jnp.zeros(acc_ref.shape, F32)
            qv = q_ref[0]
            for n in range(N_KV_HEADS):
                for g in range(KV_GROUP):
                    hd = n * KV_GROUP + g
                    qs_ref[n, g * tq:(g + 1) * tq, :] = qv[:, hd * HEAD_DIM:(hd + 1) * HEAD_DIM]

        key = keys_ref[kb]
        thr = thr_ref[...]
        kpos = kb * tk + lax.broadcasted_iota(I32, (tq, tk), 1)
        keep = jnp.logical_or(key > thr, jnp.logical_and(key == thr, kpos <= pcut_ref[...]))
        bias = jnp.where(keep, 0.0, NEG_BIG).astype(F32)
        bias4 = jnp.concatenate([bias] * KV_GROUP, axis=0)
        kv = k_ref[0]
        vv = v_ref[0]
        ones = jnp.ones((tk, HEAD_DIM), BF16)
        ns = range(N_KV_HEADS)
        lgs = [lax.dot_general(qs_ref[n], kv[:, n * HEAD_DIM:(n + 1) * HEAD_DIM], (((1,), (1,)), ((), ())),
                               preferred_element_type=F32) + bias4 for n in ns]
        ps, alphas = [], []
        for n in ns:
            cols = [lgs[n][:, c * LANES:(c + 1) * LANES] for c in range(ncol)]
            mx = cols[0]
            for c in range(1, ncol):
                mx = jnp.maximum(mx, cols[c])
            m_prev = m_ref[n]
            m_new = jnp.maximum(m_prev, jnp.max(mx, axis=1, keepdims=True))
            alphas.append(jnp.exp2(m_prev - m_new))
            ps.append(jnp.concatenate([jnp.exp2(cols[c] - m_new).astype(BF16) for c in range(ncol)], axis=1))
            m_ref[n] = m_new
        for n in ns:
            ve = jnp.concatenate([vv[:, n * HEAD_DIM:(n + 1) * HEAD_DIM], ones], axis=1)
            pv = jnp.dot(ps[n], ve, preferred_element_type=F32)
            acc_ref[n] = jnp.concatenate([alphas[n], alphas[n]], axis=1) * acc_ref[n] + pv

        @pl.when(kb == nkb - 1)
        def _finish():
            for n in range(N_KV_HEADS):
                a = acc_ref[n]
                o = a[:, :HEAD_DIM] / a[:, HEAD_DIM:]
                for g in range(KV_GROUP):
                    hd = n * KV_GROUP + g
                    o_ref[0, :, hd * HEAD_DIM:(hd + 1) * HEAD_DIM] = o[g * tq:(g + 1) * tq, :].astype(BF16)


def _dsa_tables(batch, t, tq, tk, past, n_keys):
    rows = []
    for b in range(batch):
        for qb in range(t // tq):
            last_q = past + qb * tq + tq - 1
            limit = min(n_keys, (last_q // CHUNK + 1) * CHUNK)
            nkb = -(-limit // tk)
            for ph in range(2):
                for kb in range(nkb):
                    ikb = kb if ph == 0 else nkb - 1
                    kvb = 0 if ph == 0 else kb
                    rows.append((b, qb, ph, kb, nkb, ikb, kvb))
    tab = np.asarray(rows, dtype=np.int32).T
    return [jnp.asarray(tab[r]) for r in range(tab.shape[0])]


def dsa_attention(iq, iw, ik2, q, k, v, *, n_keys, past, tq, tk):
    batch, t, _ = iq.shape
    lp = ik2.shape[1]
    assert t % tq == 0 and lp % tk == 0 and tk % LANES == 0
    topk = min(IDX_TOPK_MAX, n_keys // 4)
    tables = _dsa_tables(batch, t, tq, tk, past, n_keys)
    nsteps = int(tables[0].shape[0])
    nkb_max = lp // tk
    kern = functools.partial(_dsa_kernel, tq=tq, tk=tk, n_keys=n_keys, past=past, topk=topk,
                             pos_bits=max(1, (lp - 1).bit_length()))
    qmap = lambda s, tb, tqb, tph, tkb, tnkb, tikb, tkvb: (tb[s], tqb[s], 0)
    grid_spec = pltpu.PrefetchScalarGridSpec(
        num_scalar_prefetch=7,
        grid=(nsteps,),
        in_specs=[
            pl.BlockSpec((1, tq, IDX_WIDTH), qmap),
            pl.BlockSpec((1, tq, N_IDX_HEADS), qmap),
            pl.BlockSpec((1, tk, 2 * LANES), lambda s, tb, tqb, tph, tkb, tnkb, tikb, tkvb: (tb[s], tikb[s], 0)),
            pl.BlockSpec((1, tq, ATTN_WIDTH), qmap),
            pl.BlockSpec((1, tk, KV_WIDTH), lambda s, tb, tqb, tph, tkb, tnkb, tikb, tkvb: (tb[s], tkvb[s], 0)),
            pl.BlockSpec((1, tk, KV_WIDTH), lambda s, tb, tqb, tph, tkb, tnkb, tikb, tkvb: (tb[s], tkvb[s], 0)),
        ],
        out_specs=pl.BlockSpec((1, tq, ATTN_WIDTH), qmap),
        scratch_shapes=[
            pltpu.VMEM((nkb_max, tq, tk), I32),
            pltpu.VMEM((tq, tk), F32),
            pltpu.VMEM((tq, LANES), I32),
            pltpu.VMEM((tq, 1), I32),
            pltpu.VMEM((tq, 1), I32),
            pltpu.VMEM((N_IDX_HEADS, tq, LANES), F32),
            pltpu.VMEM((N_IDX_PAIRS * tq, LANES), BF16),
            pltpu.VMEM((N_KV_HEADS, KV_GROUP * tq, HEAD_DIM), BF16),
            pltpu.VMEM((N_KV_HEADS, KV_GROUP * tq, LANES), F32),
            pltpu.VMEM((N_KV_HEADS, KV_GROUP * tq, 2 * HEAD_DIM), F32),
        ],
    )
    return pl.pallas_call(
        kern,
        grid_spec=grid_spec,
        out_shape=jax.ShapeDtypeStruct((batch, t, ATTN_WIDTH), BF16),
        compiler_params=_cparams(("arbitrary",)),
        name="dsa_attention",
    )(*tables, iq, iw, ik2, q, k, v)


def _mm(a, b):
    return jnp.dot(a.astype(BF16), b.astype(BF16), preferred_element_type=F32)


def _mm_nt(a, b):
    return lax.dot_general(a.astype(BF16), b.astype(BF16), (((1,), (1,)), ((), ())), preferred_element_type=F32)


def _mm_tn(a, b):
    return lax.dot_general(a.astype(BF16), b.astype(BF16), (((0,), (0,)), ((), ())), preferred_element_type=F32)


def _delta_kernel(dq_ref, dk_ref, dv_ref, z_ref, bq_ref, bk_ref, bv_ref, cq_ref, ck_ref, cv_ref,
                  gc_ref, gr_ref, b_ref, gn_ref, s0_ref, o_ref, s_ref, eq_ref, ek_ref, ev_ref,
                  *, chunk, n_double):
    first = pl.program_id(1) == 0

    @pl.when(first)
    def _():
        s_ref[...] = s0_ref[...]

    def conv_silu(x_ref, buf_ref, cw_ref, e_ref):
        @pl.when(first)
        def _():
            e_ref[0:SUBLANES, :] = buf_ref[0]

        e_ref[SUBLANES:SUBLANES + chunk, :] = x_ref[0]
        w = cw_ref[...]
        y = e_ref[SUBLANES:SUBLANES + chunk, :] * w[DN_CONV - 1:DN_CONV, :]
        for i in range(DN_CONV - 1):
            off = SUBLANES - (DN_CONV - 1) + i
            y = y + e_ref[off:off + chunk, :] * w[i:i + 1, :]
        e_ref[0:SUBLANES, :] = e_ref[chunk:chunk + SUBLANES, :]
        return y * jax.nn.sigmoid(y)

    yq = conv_silu(dq_ref, bq_ref, cq_ref, eq_ref)
    yk = conv_silu(dk_ref, bk_ref, ck_ref, ek_ref)
    yv = conv_silu(dv_ref, bv_ref, cv_ref, ev_ref)

    row = lax.broadcasted_iota(I32, (chunk, chunk), 0)
    col = lax.broadcasted_iota(I32, (chunk, chunk), 1)
    eye = (row == col).astype(F32)
    gcs = gc_ref[0]
    grs = gr_ref[0, 0]
    betas = b_ref[0]
    gn = gn_ref[...]
    hs = range(DN_HEADS)
    sl = [slice(h * DN_HEAD_DIM, (h + 1) * DN_HEAD_DIM) for h in hs]

    def l2n(x):
        return x * lax.rsqrt(jnp.sum(x * x, axis=-1, keepdims=True) + EPS)

    kh = [l2n(yk[:, sl[h]]) for h in hs]
    qh = [l2n(yq[:, sl[h]]) * (DN_HEAD_DIM ** -0.5) for h in hs]
    gc = [gcs[:, h:h + 1] for h in hs]
    beta = [betas[:, h:h + 1] for h in hs]
    decay = [jnp.exp(jnp.where(row >= col, gc[h] - grs[h:h + 1, :], -jnp.inf)) for h in hs]
    kk = [_mm_nt(kh[h], kh[h]) for h in hs]
    qk = [_mm_nt(qh[h], kh[h]) * decay[h] for h in hs]
    x = [-jnp.where(row > col, beta[h] * kk[h] * decay[h], 0.0) for h in hs]
    parts = [x]
    for _ in range(n_double):
        x = [_mm(x[h], x[h]) for h in hs]
        parts.append(x)
    while len(parts) > 1:
        nxt = [[parts[i][h] + parts[i + 1][h] + _mm(parts[i][h], parts[i + 1][h]) for h in hs]
               for i in range(0, len(parts) - 1, 2)]
        if len(parts) % 2:
            nxt.append(parts[-1])
        parts = nxt
    t = [eye + parts[0][h] for h in hs]
    eg = [jnp.exp(gc[h]) for h in hs]
    uw = [_mm(t[h], jnp.concatenate([yv[:, sl[h]] * beta[h], kh[h] * (beta[h] * eg[h])], axis=1)) for h in hs]
    s = [s_ref[0, h] for h in hs]
    ws_qs = [_mm(jnp.concatenate([uw[h][:, DN_HEAD_DIM:], qh[h] * eg[h]], axis=0), s[h]) for h in hs]
    v_new = [uw[h][:, :DN_HEAD_DIM] - ws_qs[h][:chunk] for h in hs]
    gl = [gc[h][chunk - 1:chunk, :] for h in hs]
    o = [ws_qs[h][chunk:] + _mm(qk[h], v_new[h]) for h in hs]
    upd = [_mm_tn(kh[h] * jnp.exp(gl[h] - gc[h]), v_new[h]) for h in hs]
    for h in hs:
        s_ref[0, h] = s[h] * jnp.exp(gl[h]) + upd[h]
        ms = jnp.mean(o[h] * o[h], axis=-1, keepdims=True)
        on = o[h] * lax.rsqrt(ms + EPS) * gn
        z = z_ref[0, :, sl[h]]
        o_ref[0, :, sl[h]] = (on * (z * jax.nn.sigmoid(z))).astype(BF16)


def delta_rule(proj, dn_buf, conv_w, gc, beta, gnorm, state, chunk):
    batch, t, _ = proj.shape
    n = t // chunk
    n_double = max(0, (chunk - 1).bit_length() - 1)
    gr = jnp.swapaxes(gc.reshape(batch, n, chunk, DN_HEADS), 2, 3)
    buf8 = jnp.pad(dn_buf, ((0, 0), (SUBLANES - (DN_CONV - 1), 0), (0, 0)))
    kern = functools.partial(_delta_kernel, chunk=chunk, n_double=n_double)
    pcol = lambda c: pl.BlockSpec((1, chunk, DN_WIDTH), lambda b, i, c=c: (b, i, c // DN_WIDTH))
    bcol = lambda j: pl.BlockSpec((1, SUBLANES, DN_WIDTH), lambda b, i, j=j: (b, 0, j))
    wcol = lambda j: pl.BlockSpec((DN_CONV, DN_WIDTH), lambda b, i, j=j: (0, j))
    wide = pl.BlockSpec((1, chunk, DN_WIDTH), lambda b, i: (b, i, 0))
    narrow = pl.BlockSpec((1, chunk, DN_HEADS), lambda b, i: (b, i, 0))
    st = pl.BlockSpec((1, DN_HEADS, DN_HEAD_DIM, DN_HEAD_DIM), lambda b, i: (b, 0, 0, 0))
    return pl.pallas_call(
        kern,
        grid=(batch, n),
        in_specs=[pcol(COL_DQ), pcol(COL_DK), pcol(COL_DV), pcol(COL_DZ),
                  bcol(0), bcol(1), bcol(2), wcol(0), wcol(1), wcol(2),
                  narrow,
                  pl.BlockSpec((1, 1, DN_HEADS, chunk), lambda b, i: (b, i, 0, 0)),
                  narrow,
                  pl.BlockSpec((1, DN_HEAD_DIM), lambda b, i: (0, 0)),
                  st],
        out_specs=[wide, st],
        out_shape=[jax.ShapeDtypeStruct((batch, t, DN_WIDTH), BF16),
                   jax.ShapeDtypeStruct(state.shape, F32)],
        scratch_shapes=[pltpu.VMEM((chunk + SUBLANES, DN_WIDTH), F32)] * 3,
        compiler_params=_cparams(("parallel", "arbitrary")),
        name="delta_rule",
    )(proj, proj, proj, proj, buf8, buf8, buf8, conv_w, conv_w, conv_w,
      gc, gr, beta, gnorm.reshape(1, DN_HEAD_DIM), state)


def _layer(x, past_k, past_v, past_ik, dn_buf, dn_state, ffn_buf, lw, *, attn_tiles):
    (norm1_g, w_in, dn_conv_w, dn_a_log, dn_dt_bias, dn_norm_g, w_out, norm2_g, ffn_w_up, ffn_conv_w,
     ffn_w_down) = lw
    b, t, d = x.shape
    past = past_k.shape[1]
    n_keys = past + t
    x2 = x.reshape(b * t, d)

    proj = norm_matmul(x2, norm1_g, w_in)
    pos = jnp.tile(past + jnp.arange(t, dtype=I32), b)
    q_bf, iq_bf, k_f32, k_bf, v_bf, ik2_bf, small = attn_prep(proj, pos)
    proj3 = proj.reshape(b, t, IN_COLS)
    small = small.reshape(b, t, LANES)
    new_k = k_f32.reshape(b, t, N_KV_HEADS, HEAD_DIM)
    new_v = proj3[:, :, COL_AV:COL_AV + KV_WIDTH].reshape(b, t, N_KV_HEADS, HEAD_DIM)
    new_ik = small[:, :, SMALL_IK:SMALL_IK + IDX_HEAD_DIM]
    i_w = small[:, :, SMALL_IW:SMALL_IW + N_IDX_HEADS]

    tq, tk = attn_tiles
    lp = -(-n_keys // tk) * tk

    def with_past(old_bf, new_bf):
        allk = jnp.concatenate([old_bf, new_bf.reshape(b, t, -1)], axis=1)
        return jnp.pad(allk, ((0, 0), (0, lp - n_keys), (0, 0)))

    pik = past_ik.astype(BF16)
    zpad = jnp.zeros_like(pik)
    past_ik2 = jnp.concatenate([pik, zpad, zpad, pik], axis=-1)
    attn = dsa_attention(
        iq_bf.reshape(b, t, IDX_WIDTH), i_w, with_past(past_ik2, ik2_bf), q_bf.reshape(b, t, ATTN_WIDTH),
        with_past(past_k.reshape(b, past, KV_WIDTH).astype(BF16), k_bf),
        with_past(past_v.reshape(b, past, KV_WIDTH).astype(BF16), v_bf),
        n_keys=n_keys, past=past, tq=tq, tk=tk)

    new_dn_buf = jnp.concatenate([dn_buf, proj3[:, :, COL_DQ:COL_DQ + 3 * DN_WIDTH]], axis=1)[:, t:] if t < DN_CONV - 1 \
        else proj3[:, t - (DN_CONV - 1):, COL_DQ:COL_DQ + 3 * DN_WIDTH]
    beta = jax.nn.sigmoid(small[:, :, SMALL_DB:SMALL_DB + DN_HEADS])
    g = -jnp.exp(dn_a_log) * jax.nn.softplus(small[:, :, SMALL_DA:SMALL_DA + DN_HEADS] + dn_dt_bias)
    chunk = CHUNK if t % CHUNK == 0 else t
    gc = jnp.cumsum(g.reshape(b, t // chunk, chunk, DN_HEADS), axis=2).reshape(b, t, DN_HEADS)
    o_dn, new_state = delta_rule(proj3, dn_buf, dn_conv_w, gc, beta, dn_norm_g, dn_state, chunk)

    x2 = matmul_res([attn.reshape(b * t, ATTN_WIDTH), o_dn.reshape(b * t, DN_WIDTH)], w_out, x2)

    f2 = ffn_w_up.shape[1]
    xt = jnp.swapaxes(x2.reshape(b, t, d), 0, 1).reshape(t * b, d)
    halo = max(SUBLANES, (FFN_CONV - 1) * b)
    hist = jnp.swapaxes(ffn_buf, 0, 1).reshape((FFN_CONV - 1) * b, f2)
    hist = ffn_interleave(jnp.pad(hist, ((halo - hist.shape[0], 0), (0, 0))))
    act, tail = ffn_up(xt, norm2_g, ffn_w_up, ffn_conv_w, hist, step=b)
    tail = ffn_deinterleave(tail[halo - (FFN_CONV - 1) * b:])
    new_ffn_buf = jnp.swapaxes(tail.reshape(FFN_CONV - 1, b, f2), 0, 1)
    act = jnp.swapaxes(act.reshape(t, b, -1), 0, 1).reshape(b * t, -1)
    x2 = matmul_res([act], ffn_w_down, x2)
    return x2.reshape(b, t, d), new_k, new_v, new_ik, new_dn_buf, new_state, new_ffn_buf


def kernel(x_prompt, x_sample, cache_k, cache_v, cache_kidx, state_dn_conv, state_dn, state_ffn_conv,
           norm1_g, w_in, dn_conv_w, dn_a_log, dn_dt_bias, dn_norm_g, w_out, norm2_g,
           ffn_w_up, ffn_conv_w, ffn_w_down, final_g):
    depth = w_in.shape[0]
    bp, tp, d = x_prompt.shape
    bs, ts, _ = x_sample.shape
    f2 = ffn_w_up.shape[2]
    zk = jnp.zeros((bp, 0, N_KV_HEADS, HEAD_DIM), F32)
    zik = jnp.zeros((bp, 0, IDX_HEAD_DIM), F32)
    z_dn_buf = jnp.zeros((bp, DN_CONV - 1, 3 * DN_WIDTH), F32)
    z_dn_state = jnp.zeros((bp, DN_HEADS, DN_HEAD_DIM, DN_HEAD_DIM), F32)
    z_ffn_buf = jnp.zeros((bp, FFN_CONV - 1, f2), F32)

    tq_p = _pick(tp, (256, 128, 64, 32, 16))
    tk_p = _pick(tp, (512, 256, 128))
    n_keys_s = cache_k.shape[2] + ts
    tk_s = -(-n_keys_s // LANES) * LANES

    xp, xs = x_prompt, x_sample
    st_p, st_s = [], []
    for l in range(depth):
        lw = (norm1_g[l], _regroup_in_proj(w_in[l]), dn_conv_w[l], dn_a_log[l], dn_dt_bias[l],
              dn_norm_g[l], w_out[l].astype(BF16), norm2_g[l], ffn_w_up[l].astype(BF16),
              ffn_interleave(ffn_conv_w[l]), ffn_w_down[l].astype(BF16))
        xp, *sp = _layer(xp, zk, zk, zik, z_dn_buf, z_dn_state, z_ffn_buf, lw, attn_tiles=(tq_p, tk_p))
        xs, *ss = _layer(xs, cache_k[l], cache_v[l], cache_kidx[l], state_dn_conv[l], state_dn[l],
                         state_ffn_conv[l], lw, attn_tiles=(ts, tk_s))
        st_p.append(sp)
        st_s.append(ss)

    y_prompt = rms_norm_rows(xp.reshape(bp * tp, d), final_g).reshape(bp, tp, d)
    y_sample = rms_norm_rows(xs.reshape(bs * ts, d), final_g).reshape(bs, ts, d)

    def stacked(states, i):
        return jnp.stack([s[i] for s in states], axis=0)

    return (y_prompt, y_sample,
            *[stacked(st_p, i) for i in range(6)],
            *[stacked(st_s, i) for i in range(6)])
```

```python
import functools

import numpy as np
import jax
import jax.numpy as jnp
from jax import lax
from jax.experimental import pallas as pl
from jax.experimental.pallas import tpu as pltpu

F32 = jnp.float32
BF16 = jnp.bfloat16
I32 = jnp.int32

CHUNK = 64
CHUNK_SHIFT = CHUNK.bit_length() - 1
HEAD_DIM = 128
N_HEADS = 16
N_KV_HEADS = 4
KV_GROUP = N_HEADS // N_KV_HEADS
ATTN_WIDTH = N_HEADS * HEAD_DIM
KV_WIDTH = N_KV_HEADS * HEAD_DIM
N_IDX_HEADS = 32
IDX_HEAD_DIM = 64
IDX_WIDTH = N_IDX_HEADS * IDX_HEAD_DIM
IDX_SCALE = IDX_WIDTH ** -0.5
IDX_TOPK_MAX = 256
DN_HEADS = 16
DN_HEAD_DIM = 128
DN_WIDTH = DN_HEADS * DN_HEAD_DIM
DN_CONV = 4
FFN_CONV = 3
ROPE_THETA = 10000.0
EPS = 1e-6
LOG2E = 1.4426950408889634

LANES = 128
SUBLANES = 8
VMEM_LIMIT = 56 * 1024 * 1024

INT_MIN = -(2 ** 31)
INT_MAX = 2 ** 31 - 1
NEG_BIG = -1e30

COL_AQ, COL_IQ, COL_DQ, COL_DK, COL_DV, COL_DZ = (i * ATTN_WIDTH for i in range(6))
COL_AK = 6 * ATTN_WIDTH
COL_AV = COL_AK + KV_WIDTH
COL_SMALL = COL_AV + KV_WIDTH
SMALL_IK, SMALL_IW, SMALL_DB, SMALL_DA = 0, 64, 96, 112
IN_TILE = 768
IN_COLS = -(-(COL_SMALL + LANES) // IN_TILE) * IN_TILE
assert ATTN_WIDTH == IDX_WIDTH == DN_WIDTH


def _regroup_in_proj(w):
    splits = (ATTN_WIDTH, KV_WIDTH, KV_WIDTH, IDX_WIDTH, IDX_HEAD_DIM, N_IDX_HEADS,
              DN_WIDTH, DN_WIDTH, DN_WIDTH, DN_HEADS, DN_HEADS, DN_WIDTH)
    starts = np.concatenate([[0], np.cumsum(splits)])
    seg = {n: (int(starts[i]), int(starts[i + 1])) for i, n in enumerate(
        ("a_q", "a_k", "a_v", "i_q", "i_k", "i_w", "d_q", "d_k", "d_v", "d_b", "d_a", "d_z"))}
    order = ("a_q", "i_q", "d_q", "d_k", "d_v", "d_z", "a_k", "a_v", "i_k", "i_w", "d_b", "d_a")
    parts = [w[:, seg[n][0]:seg[n][1]].astype(BF16) for n in order]
    parts.append(jnp.zeros((w.shape[0], IN_COLS - int(starts[-1])), BF16))
    return jnp.concatenate(parts, axis=1)


def _cparams(sem):
    return pltpu.CompilerParams(dimension_semantics=sem, vmem_limit_bytes=VMEM_LIMIT)


def _pick(n, prefs):
    for p in prefs:
        if n % p == 0:
            return p
    raise ValueError(f"no tile in {prefs} divides {n}")


ROW_TILES = (512, 256, 128, 64, 32, 16, 8)


def _norm_matmul_kernel(x_ref, g_ref, w_ref, o_ref, h_ref):
    @pl.when(pl.program_id(1) == 0)
    def _():
        x = x_ref[...]
        ms = jnp.mean(x * x, axis=-1, keepdims=True)
        h_ref[...] = (x * lax.rsqrt(ms + EPS) * g_ref[...]).astype(BF16)

    o_ref[...] = jnp.dot(h_ref[...], w_ref[...], preferred_element_type=F32)


def norm_matmul(x, g, w):
    n, d = x.shape
    cols = w.shape[1]
    tm = _pick(n, ROW_TILES)
    tn = _pick(cols, (IN_TILE, 512, 256, 128))
    return pl.pallas_call(
        _norm_matmul_kernel,
        grid=(n // tm, cols // tn),
        in_specs=[pl.BlockSpec((tm, d), lambda i, j: (i, 0)),
                  pl.BlockSpec((1, d), lambda i, j: (0, 0)),
                  pl.BlockSpec((d, tn), lambda i, j: (0, j))],
        out_specs=pl.BlockSpec((tm, tn), lambda i, j: (i, j)),
        out_shape=jax.ShapeDtypeStruct((n, cols), F32),
        scratch_shapes=[pltpu.VMEM((tm, d), BF16)],
        compiler_params=_cparams(("parallel", "arbitrary")),
        name="norm_matmul",
    )(x, g.reshape(1, d), w)


def _matmul_res_kernel(*refs, n_lhs):
    a_refs, w_refs, x_ref, o_ref = refs[:n_lhs], refs[n_lhs:2 * n_lhs], refs[2 * n_lhs], refs[2 * n_lhs + 1]
    acc = x_ref[...]
    for a_ref, w_ref in zip(a_refs, w_refs):
        acc = acc + jnp.dot(a_ref[...], w_ref[...], preferred_element_type=F32)
    o_ref[...] = acc


def matmul_res(lhs, w, x):
    n, k = lhs[0].shape
    cols = w.shape[1]
    tm = _pick(n, ROW_TILES)
    tn = _pick(cols, (256,) if k > 8192 else (1024, 512, 256, 128))
    kern = functools.partial(_matmul_res_kernel, n_lhs=len(lhs))
    return pl.pallas_call(
        kern,
        grid=(n // tm, cols // tn),
        in_specs=([pl.BlockSpec((tm, k), lambda i, j: (i, 0)) for _ in lhs]
                  + [pl.BlockSpec((k, tn), lambda i, j, r=r: (r, j)) for r in range(len(lhs))]
                  + [pl.BlockSpec((tm, tn), lambda i, j: (i, j))]),
        out_specs=pl.BlockSpec((tm, tn), lambda i, j: (i, j)),
        out_shape=jax.ShapeDtypeStruct((n, cols), F32),
        compiler_params=_cparams(("parallel", "arbitrary")),
        name="matmul_res",
    )(*lhs, *([w] * len(lhs)), x)


def _rms_norm_kernel(x_ref, g_ref, o_ref):
    x = x_ref[...]
    ms = jnp.mean(x * x, axis=-1, keepdims=True)
    o_ref[...] = x * lax.rsqrt(ms + EPS) * g_ref[...]


def rms_norm_rows(x, g):
    n, d = x.shape
    tm = _pick(n, ROW_TILES)
    return pl.pallas_call(
        _rms_norm_kernel,
        grid=(n // tm,),
        in_specs=[pl.BlockSpec((tm, d), lambda i: (i, 0)), pl.BlockSpec((1, d), lambda i: (0, 0))],
        out_specs=pl.BlockSpec((tm, d), lambda i: (i, 0)),
        out_shape=jax.ShapeDtypeStruct((n, d), F32),
        compiler_params=_cparams(("parallel",)),
        name="rms_norm",
    )(x, g.reshape(1, d))


FFN_TILE = 256


def ffn_interleave(a):
    f = a.shape[-1] // 2
    lead = a.shape[:-1]
    return jnp.swapaxes(a.reshape(*lead, 2, f // FFN_TILE, FFN_TILE), -3, -2).reshape(*lead, 2 * f)


def ffn_deinterleave(a):
    f = a.shape[-1] // 2
    lead = a.shape[:-1]
    return jnp.swapaxes(a.reshape(*lead, f // FFN_TILE, 2, FFN_TILE), -3, -2).reshape(*lead, 2 * f)


def _ffn_up_kernel(x_ref, g_ref, wg_ref, wv_ref, c_ref, hist_ref, a_ref, st_ref, h_ref, e_ref, cy_ref,
                   *, tm, tn, step, halo):
    i = pl.program_id(0)
    j = pl.program_id(1)

    @pl.when(j == 0)
    def _():
        x = x_ref[...]
        ms = jnp.mean(x * x, axis=-1, keepdims=True)
        h_ref[...] = (x * lax.rsqrt(ms + EPS) * g_ref[...]).astype(BF16)

    h = h_ref[...]
    up = jnp.concatenate([jnp.dot(h, wg_ref[...], preferred_element_type=F32),
                          jnp.dot(h, wv_ref[...], preferred_element_type=F32)], axis=1)
    e_ref[halo:halo + tm, :] = up

    @pl.when(i == 0)
    def _():
        e_ref[0:halo, :] = hist_ref[...]

    @pl.when(i > 0)
    def _():
        e_ref[0:halo, :] = cy_ref[j]

    c = c_ref[...]
    y = (e_ref[halo - 2 * step:halo - 2 * step + tm, :] * c[0:1, :]
         + e_ref[halo - step:halo - step + tm, :] * c[1:2, :]
         + up * c[2:3, :])
    tail = e_ref[tm:tm + halo, :]
    cy_ref[j] = tail
    st_ref[...] = tail
    yg = y[:, :tn]
    a_ref[...] = (yg * jax.nn.sigmoid(yg) * y[:, tn:]).astype(BF16)


def ffn_up(x, g, w_up, conv_w, hist, step):
    n, d = x.shape
    f2 = w_up.shape[1]
    f = f2 // 2
    halo = hist.shape[0]
    tm = _pick(n, ROW_TILES)
    tn = FFN_TILE
    nj = f // tn
    kern = functools.partial(_ffn_up_kernel, tm=tm, tn=tn, step=step, halo=halo)
    act, tail = pl.pallas_call(
        kern,
        grid=(n // tm, nj),
        in_specs=[pl.BlockSpec((tm, d), lambda i, j: (i, 0)),
                  pl.BlockSpec((1, d), lambda i, j: (0, 0)),
                  pl.BlockSpec((d, tn), lambda i, j: (0, j)),
                  pl.BlockSpec((d, tn), lambda i, j: (0, nj + j)),
                  pl.BlockSpec((FFN_CONV, 2 * tn), lambda i, j: (0, j)),
                  pl.BlockSpec((halo, 2 * tn), lambda i, j: (0, j))],
        out_specs=[pl.BlockSpec((tm, tn), lambda i, j: (i, j)),
                   pl.BlockSpec((None, halo, 2 * tn), lambda i, j: (i, 0, j))],
        out_shape=[jax.ShapeDtypeStruct((n, f), BF16),
                   jax.ShapeDtypeStruct((n // tm, halo, f2), F32)],
        scratch_shapes=[pltpu.VMEM((tm, d), BF16),
                        pltpu.VMEM((tm + halo, 2 * tn), F32),
                        pltpu.VMEM((nj, halo, 2 * tn), F32)],
        compiler_params=_cparams(("arbitrary", "arbitrary")),
        name="ffn_up",
    )(x, g.reshape(1, d), w_up, w_up, conv_w, hist)
    return act, tail[-1]


def _attn_prep_kernel(aq_ref, iq_ref, ak_ref, av_ref, sm_ref, cosa_ref, sina_ref, cosi_ref, sini_ref,
                      q_ref, iqo_ref, kf_ref, kb_ref, vb_ref, ik2_ref, smo_ref, *, tm):
    cosa, sina = cosa_ref[...], sina_ref[...]
    cosi, sini = cosi_ref[...], sini_ref[...]
    lane = lax.broadcasted_iota(I32, (tm, LANES), 1)
    first_half = (lane & (IDX_HEAD_DIM - 1)) < IDX_HEAD_DIM // 2

    def rot_head(x):
        return x * cosa + pltpu.roll(x, HEAD_DIM // 2, 1) * sina

    def rot_idx(x):
        partner = jnp.where(first_half, pltpu.roll(x, LANES - IDX_HEAD_DIM // 2, 1), pltpu.roll(x, IDX_HEAD_DIM // 2, 1))
        return x * cosi + partner * sini

    qscale = (HEAD_DIM ** -0.5) * LOG2E
    for h in range(N_HEADS):
        sl = slice(h * LANES, (h + 1) * LANES)
        q_ref[:, sl] = (rot_head(aq_ref[:, sl]) * qscale).astype(BF16)
        iqo_ref[:, sl] = rot_idx(iq_ref[:, sl]).astype(BF16)
    for n in range(N_KV_HEADS):
        sl = slice(n * LANES, (n + 1) * LANES)
        kr = rot_head(ak_ref[:, sl])
        kf_ref[:, sl] = kr
        kb_ref[:, sl] = kr.astype(BF16)
    vb_ref[...] = av_ref[...].astype(BF16)
    sm = sm_ref[...]
    ik = jnp.where(lane < SMALL_IW, rot_idx(sm), 0.0)
    smo_ref[...] = jnp.where(lane < SMALL_IW, ik, jnp.where(lane < SMALL_DB, sm * IDX_SCALE, sm))
    ik2_ref[:, :LANES] = ik.astype(BF16)
    ik2_ref[:, LANES:] = pltpu.roll(ik, IDX_HEAD_DIM, 1).astype(BF16)


def attn_prep(proj, pos):
    n = proj.shape[0]
    tm = _pick(n, (256, 128, 64, 32, 16, 8))
    posf = pos.astype(F32)[:, None]

    def tables(half):
        inv_freq = jnp.float32(ROPE_THETA) ** (-jnp.arange(half, dtype=F32) / half)
        ang = posf * inv_freq[None, :]
        cos, sin = jnp.cos(ang), jnp.sin(ang)
        rep = LANES // (2 * half)
        return jnp.tile(jnp.concatenate([cos, cos], axis=1), (1, rep)), jnp.tile(jnp.concatenate([-sin, sin], axis=1), (1, rep))

    cosa, sina = tables(HEAD_DIM // 2)
    cosi, sini = tables(IDX_HEAD_DIM // 2)
    wide = lambda c: pl.BlockSpec((tm, ATTN_WIDTH), lambda i, c=c: (i, c // ATTN_WIDTH))
    kvw = lambda c: pl.BlockSpec((tm, KV_WIDTH), lambda i, c=c: (i, c // KV_WIDTH))
    tab = pl.BlockSpec((tm, LANES), lambda i: (i, 0))
    row = lambda w: pl.BlockSpec((tm, w), lambda i: (i, 0))
    return pl.pallas_call(
        functools.partial(_attn_prep_kernel, tm=tm),
        grid=(n // tm,),
        in_specs=[wide(COL_AQ), wide(COL_IQ), kvw(COL_AK), kvw(COL_AV),
                  pl.BlockSpec((tm, LANES), lambda i: (i, COL_SMALL // LANES)), tab, tab, tab, tab],
        out_specs=[row(ATTN_WIDTH), row(IDX_WIDTH), row(KV_WIDTH), row(KV_WIDTH), row(KV_WIDTH), row(2 * LANES), row(LANES)],
        out_shape=[jax.ShapeDtypeStruct((n, ATTN_WIDTH), BF16),
                   jax.ShapeDtypeStruct((n, IDX_WIDTH), BF16),
                   jax.ShapeDtypeStruct((n, KV_WIDTH), F32),
                   jax.ShapeDtypeStruct((n, KV_WIDTH), BF16),
                   jax.ShapeDtypeStruct((n, KV_WIDTH), BF16),
                   jax.ShapeDtypeStruct((n, 2 * LANES), BF16),
                   jax.ShapeDtypeStruct((n, LANES), F32)],
        compiler_params=_cparams(("parallel",)),
        name="attn_prep",
    )(proj, proj, proj, proj, proj, cosa, sina, cosi, sini)


IDX_PAIR_GROUP = 8
N_IDX_PAIRS = N_IDX_HEADS // 2


def _dsa_kernel(tb_ref, tqb_ref, tph_ref, tkb_ref, tnkb_ref, tikb_ref, tkvb_ref,
                iq_ref, iw_ref, ik_ref, q_ref, k_ref, v_ref, o_ref,
                keys_ref, sc_ref, kmax_ref, k2nd_ref, thr_ref, pcut_ref, wb_ref, iqs_ref, qs_ref, m_ref, acc_ref,
                *, tq, tk, n_keys, past, topk, pos_bits):
    s = pl.program_id(0)
    ph = tph_ref[s]
    kb = tkb_ref[s]
    nkb = tnkb_ref[s]
    q0 = tqb_ref[s] * tq + past
    ncol = tk // LANES
    grows = IDX_PAIR_GROUP * tq

    @pl.when(ph == 0)
    def _scores():
        @pl.when(kb == 0)
        def _():
            w = iw_ref[0]
            for h in range(N_IDX_HEADS):
                wb_ref[h] = jnp.broadcast_to(w[:, h:h + 1], (tq, LANES))
            for p in range(N_IDX_PAIRS):
                iqs_ref[p * tq:(p + 1) * tq, :] = iq_ref[0, :, p * LANES:(p + 1) * LANES]

        ik2 = ik_ref[0]
        ik_even, ik_odd = ik2[:, :LANES], ik2[:, LANES:]
        sc_ref[...] = jnp.zeros((tq, tk), F32)

        def group(gi, carry):
            lhs = iqs_ref[pl.ds(pl.multiple_of(gi * grows, grows), grows), :]
            raws = [lax.dot_general(lhs, kk, (((1,), (1,)), ((), ())), preferred_element_type=F32)
                    for kk in (ik_even, ik_odd)]
            part = None
            for j in range(IDX_PAIR_GROUP):
                for par in range(2):
                    wv = wb_ref[(gi * IDX_PAIR_GROUP + j) * 2 + par]
                    wv = jnp.concatenate([wv] * ncol, axis=1) if ncol > 1 else wv
                    term = wv * jnp.maximum(raws[par][j * tq:(j + 1) * tq, :], 0.0)
                    part = term if part is None else part + term
            sc_ref[...] += part
            return carry

        lax.fori_loop(0, N_IDX_PAIRS // IDX_PAIR_GROUP, group, 0)

        sc = sc_ref[...] + 0.0
        bits = pltpu.bitcast(sc, I32)
        key = bits ^ ((bits >> 31) & INT_MAX)
        kpos = kb * tk + lax.broadcasted_iota(I32, (tq, tk), 1)
        qpos = q0 + lax.broadcasted_iota(I32, (tq, tk), 0)
        adm = jnp.logical_and((kpos >> CHUNK_SHIFT) <= (qpos >> CHUNK_SHIFT), kpos < n_keys)
        key = jnp.where(adm, key, INT_MIN)
        keys_ref[kb] = key
        k1 = key[:, :LANES]
        k2 = jnp.full((tq, LANES), INT_MIN, I32)
        for c in range(1, ncol):
            x = key[:, c * LANES:(c + 1) * LANES]
            k2 = jnp.maximum(k2, jnp.minimum(k1, x))
            k1 = jnp.maximum(k1, x)

        @pl.when(kb == 0)
        def _():
            kmax_ref[...] = k1
            k2nd_ref[...] = k2

        @pl.when(kb > 0)
        def _():
            o1 = kmax_ref[...]
            kmax_ref[...] = jnp.maximum(o1, k1)
            k2nd_ref[...] = jnp.maximum(jnp.minimum(o1, k1), jnp.maximum(k2nd_ref[...], k2))

    @pl.when(ph == 1)
    def _attend():
        def count_rows(pred):
            def blk(b, cnt):
                hit = pred(keys_ref[b], b).astype(I32)
                for c in range(ncol):
                    cnt = cnt + hit[:, c * LANES:(c + 1) * LANES]
                return cnt

            cnt = lax.fori_loop(0, nkb, blk, jnp.zeros((tq, LANES), I32))
            return jnp.sum(cnt, axis=1, keepdims=True)

        @pl.when(kb == 0)
        def _select():
            qrow = q0 + lax.broadcasted_iota(I32, (tq, 1), 0)
            n_adm = jnp.minimum(((qrow >> CHUNK_SHIFT) + 1) << CHUNK_SHIFT, n_keys)
            keep_all = n_adm <= topk
            lo0 = jnp.where(keep_all, INT_MIN + 1,
                            jnp.maximum(jnp.min(k2nd_ref[...], axis=1, keepdims=True), INT_MIN + 1))
            hi0 = jnp.max(kmax_ref[...], axis=1, keepdims=True) + 1
            cnt0 = count_rows(lambda x, b: x >= lo0)
            done0 = jnp.logical_or(keep_all, cnt0 == topk).astype(I32)

            def bis_cond(c):
                return jnp.logical_and(c[0] < 34, c[5] > 0)

            def bis_body(c):
                it, lo, hi, cnt_lo, done, _ = c
                mid = (lo >> 1) + (hi >> 1) + (lo & hi & 1)
                tot = count_rows(lambda x, b: x >= mid)
                move = jnp.logical_and(tot >= topk, done == 0)
                shrink = jnp.logical_and(tot < topk, done == 0)
                cnt_new = jnp.where(move, tot, cnt_lo)
                done = jnp.where(jnp.logical_or(cnt_new == topk, mid == lo), 1, done)
                return (it + 1, jnp.where(move, mid, lo), jnp.where(shrink, mid, hi), cnt_new, done,
                        jnp.sum(1 - done))

            _, thr, _, cnt_thr, _, _ = lax.while_loop(
                bis_cond, bis_body, (jnp.int32(0), lo0, hi0, cnt0, done0, jnp.sum(1 - done0)))
            thr_ref[...] = thr
            pcut_ref[...] = jnp.full((tq, 1), INT_MAX, I32)

            tied = cnt_thr > topk

            @pl.when(jnp.max(tied.astype(I32)) > 0)
            def _ties():
                room = topk - count_rows(lambda x, b: x > thr)

                def pos_body(it, cut):
                    cand = cut + lax.shift_left(jnp.int32(1), pos_bits - 1 - it)

                    def pred(x, b):
                        kpos = b * tk + lax.broadcasted_iota(I32, (tq, tk), 1)
                        return jnp.logical_and(x == thr, kpos < cand)

                    return jnp.where(count_rows(pred) < room, cand, cut)

                cut = lax.fori_loop(0, pos_bits, pos_body, jnp.zeros((tq, 1), I32))
                pcut_ref[...] = jnp.where(tied, cut, INT_MAX)

            m_ref[...] = jnp.full(m_ref.shape, NEG_BIG, F32)
            acc_ref[...] = jnp.zeros(acc_ref.shape, F32)
            qv = q_ref[0]
            for n in range(N_KV_HEADS):
                for g in range(KV_GROUP):
                    hd = n * KV_GROUP + g
                    qs_ref[n, g * tq:(g + 1) * tq, :] = qv[:, hd * HEAD_DIM:(hd + 1) * HEAD_DIM]

        key = keys_ref[kb]
        thr = thr_ref[...]
        kpos = kb * tk + lax.broadcasted_iota(I32, (tq, tk), 1)
        keep = jnp.logical_or(key > thr, jnp.logical_and(key == thr, kpos <= pcut_ref[...]))
        bias = jnp.where(keep, 0.0, NEG_BIG).astype(F32)
        bias4 = jnp.concatenate([bias] * KV_GROUP, axis=0)
        kv = k_ref[0]
        vv = v_ref[0]
        ones = jnp.ones((tk, HEAD_DIM), BF16)
        ns = range(N_KV_HEADS)
        lgs = [lax.dot_general(qs_ref[n], kv[:, n * HEAD_DIM:(n + 1) * HEAD_DIM], (((1,), (1,)), ((), ())),
                               preferred_element_type=F32) + bias4 for n in ns]
        ps, alphas = [], []
        for n in ns:
            cols = [lgs[n][:, c * LANES:(c + 1) * LANES] for c in range(ncol)]
            mx = cols[0]
            for c in range(1, ncol):
                mx = jnp.maximum(mx, cols[c])
            m_prev = m_ref[n]
            m_new = jnp.maximum(m_prev, jnp.max(mx, axis=1, keepdims=True))
            alphas.append(jnp.exp2(m_prev - m_new))
            ps.append(jnp.concatenate([jnp.exp2(cols[c] - m_new).astype(BF16) for c in range(ncol)], axis=1))
            m_ref[n] = m_new
        for n in ns:
            ve = jnp.concatenate([vv[:, n * HEAD_DIM:(n + 1) * HEAD_DIM], ones], axis=1)
            pv = jnp.dot(ps[n], ve, preferred_element_type=F32)
            acc_ref[n] = jnp.concatenate([alphas[n], alphas[n]], axis=1) * acc_ref[n] + pv

        @pl.when(kb == nkb - 1)
        def _finish():
            for n in range(N_KV_HEADS):
                a = acc_ref[n]
                o = a[:, :HEAD_DIM] / a[:, HEAD_DIM:]
                for g in range(KV_GROUP):
                    hd = n * KV_GROUP + g
                    o_ref[0, :, hd * HEAD_DIM:(hd + 1) * HEAD_DIM] = o[g * tq:(g + 1) * tq, :].astype(BF16)


def _dsa_tables(batch, t, tq, tk, past, n_keys):
    rows = []
    for b in range(batch):
        for qb in range(t // tq):
            last_q = past + qb * tq + tq - 1
            limit = min(n_keys, (last_q // CHUNK + 1) * CHUNK)
            nkb = -(-limit // tk)
            for ph in range(2):
                for kb in range(nkb):
                    ikb = kb if ph == 0 else nkb - 1
                    kvb = 0 if ph == 0 else kb
                    rows.append((b, qb, ph, kb, nkb, ikb, kvb))
    tab = np.asarray(rows, dtype=np.int32).T
    return [jnp.asarray(tab[r]) for r in range(tab.shape[0])]


def dsa_attention(iq, iw, ik2, q, k, v, *, n_keys, past, tq, tk):
    batch, t, _ = iq.shape
    lp = ik2.shape[1]
    assert t % tq == 0 and lp % tk == 0 and tk % LANES == 0
    topk = min(IDX_TOPK_MAX, n_keys // 4)
    assert topk <= 2 * LANES
    tables = _dsa_tables(batch, t, tq, tk, past, n_keys)
    nsteps = int(tables[0].shape[0])
    nkb_max = lp // tk
    kern = functools.partial(_dsa_kernel, tq=tq, tk=tk, n_keys=n_keys, past=past, topk=topk,
                             pos_bits=max(1, (lp - 1).bit_length()))
    qmap = lambda s, tb, tqb, tph, tkb, tnkb, tikb, tkvb: (tb[s], tqb[s], 0)
    grid_spec = pltpu.PrefetchScalarGridSpec(
        num_scalar_prefetch=7,
        grid=(nsteps,),
        in_specs=[
            pl.BlockSpec((1, tq, IDX_WIDTH), qmap),
            pl.BlockSpec((1, tq, N_IDX_HEADS), qmap),
            pl.BlockSpec((1, tk, 2 * LANES), lambda s, tb, tqb, tph, tkb, tnkb, tikb, tkvb: (tb[s], tikb[s], 0)),
            pl.BlockSpec((1, tq, ATTN_WIDTH), qmap),
            pl.BlockSpec((1, tk, KV_WIDTH), lambda s, tb, tqb, tph, tkb, tnkb, tikb, tkvb: (tb[s], tkvb[s], 0)),
            pl.BlockSpec((1, tk, KV_WIDTH), lambda s, tb, tqb, tph, tkb, tnkb, tikb, tkvb: (tb[s], tkvb[s], 0)),
        ],
        out_specs=pl.BlockSpec((1, tq, ATTN_WIDTH), qmap),
        scratch_shapes=[
            pltpu.VMEM((nkb_max, tq, tk), I32),
            pltpu.VMEM((tq, tk), F32),
            pltpu.VMEM((tq, LANES), I32),
            pltpu.VMEM((tq, LANES), I32),
            pltpu.VMEM((tq, 1), I32),
            pltpu.VMEM((tq, 1), I32),
            pltpu.VMEM((N_IDX_HEADS, tq, LANES), F32),
            pltpu.VMEM((N_IDX_PAIRS * tq, LANES), BF16),
            pltpu.VMEM((N_KV_HEADS, KV_GROUP * tq, HEAD_DIM), BF16),
            pltpu.VMEM((N_KV_HEADS, KV_GROUP * tq, LANES), F32),
            pltpu.VMEM((N_KV_HEADS, KV_GROUP * tq, 2 * HEAD_DIM), F32),
        ],
    )
    return pl.pallas_call(
        kern,
        grid_spec=grid_spec,
        out_shape=jax.ShapeDtypeStruct((batch, t, ATTN_WIDTH), BF16),
        compiler_params=_cparams(("arbitrary",)),
        name="dsa_attention",
    )(*tables, iq, iw, ik2, q, k, v)


def _mm(a, b):
    return jnp.dot(a.astype(BF16), b.astype(BF16), preferred_element_type=F32)


def _mm_nt(a, b):
    return lax.dot_general(a.astype(BF16), b.astype(BF16), (((1,), (1,)), ((), ())), preferred_element_type=F32)


def _mm_tn(a, b):
    return lax.dot_general(a.astype(BF16), b.astype(BF16), (((0,), (0,)), ((), ())), preferred_element_type=F32)


def _delta_kernel(dq_ref, dk_ref, dv_ref, z_ref, bq_ref, bk_ref, bv_ref, cq_ref, ck_ref, cv_ref,
                  gc_ref, gr_ref, b_ref, gn_ref, s0_ref, o_ref, s_ref, eq_ref, ek_ref, ev_ref,
                  *, chunk, n_double):
    first = pl.program_id(1) == 0

    @pl.when(first)
    def _():
        s_ref[...] = s0_ref[...]

    def conv_silu(x_ref, buf_ref, cw_ref, e_ref):
        @pl.when(first)
        def _():
            e_ref[0:SUBLANES, :] = buf_ref[0]

        e_ref[SUBLANES:SUBLANES + chunk, :] = x_ref[0]
        w = cw_ref[...]
        y = e_ref[SUBLANES:SUBLANES + chunk, :] * w[DN_CONV - 1:DN_CONV, :]
        for i in range(DN_CONV - 1):
            off = SUBLANES - (DN_CONV - 1) + i
            y = y + e_ref[off:off + chunk, :] * w[i:i + 1, :]
        e_ref[0:SUBLANES, :] = e_ref[chunk:chunk + SUBLANES, :]
        return y * jax.nn.sigmoid(y)

    yq = conv_silu(dq_ref, bq_ref, cq_ref, eq_ref)
    yk = conv_silu(dk_ref, bk_ref, ck_ref, ek_ref)
    yv = conv_silu(dv_ref, bv_ref, cv_ref, ev_ref)

    row = lax.broadcasted_iota(I32, (chunk, chunk), 0)
    col = lax.broadcasted_iota(I32, (chunk, chunk), 1)
    eye = (row == col).astype(F32)
    gcs = gc_ref[0]
    grs = gr_ref[0, 0]
    betas = b_ref[0]
    gn = gn_ref[...]
    hs = range(DN_HEADS)
    sl = [slice(h * DN_HEAD_DIM, (h + 1) * DN_HEAD_DIM) for h in hs]

    def l2n(x):
        return x * lax.rsqrt(jnp.sum(x * x, axis=-1, keepdims=True) + EPS)

    kh = [l2n(yk[:, sl[h]]) for h in hs]
    qh = [l2n(yq[:, sl[h]]) * (DN_HEAD_DIM ** -0.5) for h in hs]
    gc = [gcs[:, h:h + 1] for h in hs]
    beta = [betas[:, h:h + 1] for h in hs]
    decay = [jnp.exp(jnp.where(row >= col, gc[h] - grs[h:h + 1, :], -jnp.inf)) for h in hs]
    kk = [_mm_nt(kh[h], kh[h]) for h in hs]
    qk = [_mm_nt(qh[h], kh[h]) * decay[h] for h in hs]
    x = [-jnp.where(row > col, beta[h] * kk[h] * decay[h], 0.0) for h in hs]
    parts = [x]
    for _ in range(n_double):
        x = [_mm(x[h], x[h]) for h in hs]
        parts.append(x)
    while len(parts) > 1:
        nxt = [[parts[i][h] + parts[i + 1][h] + _mm(parts[i][h], parts[i + 1][h]) for h in hs]
               for i in range(0, len(parts) - 1, 2)]
        if len(parts) % 2:
            nxt.append(parts[-1])
        parts = nxt
    t = [eye + parts[0][h] for h in hs]
    eg = [jnp.exp(gc[h]) for h in hs]
    uw = [_mm(t[h], jnp.concatenate([yv[:, sl[h]] * beta[h], kh[h] * (beta[h] * eg[h])], axis=1)) for h in hs]
    s = [s_ref[0, h] for h in hs]
    ws_qs = [_mm(jnp.concatenate([uw[h][:, DN_HEAD_DIM:], qh[h] * eg[h]], axis=0), s[h]) for h in hs]
    v_new = [uw[h][:, :DN_HEAD_DIM] - ws_qs[h][:chunk] for h in hs]
    gl = [gc[h][chunk - 1:chunk, :] for h in hs]
    o = [ws_qs[h][chunk:] + _mm(qk[h], v_new[h]) for h in hs]
    upd = [_mm_tn(kh[h] * jnp.exp(gl[h] - gc[h]), v_new[h]) for h in hs]
    for h in hs:
        s_ref[0, h] = s[h] * jnp.exp(gl[h]) + upd[h]
        ms = jnp.mean(o[h] * o[h], axis=-1, keepdims=True)
        on = o[h] * lax.rsqrt(ms + EPS) * gn
        z = z_ref[0, :, sl[h]]
        o_ref[0, :, sl[h]] = (on * (z * jax.nn.sigmoid(z))).astype(BF16)


def delta_rule(proj, dn_buf, conv_w, gc, beta, gnorm, state, chunk):
    batch, t, _ = proj.shape
    n = t // chunk
    n_double = max(0, (chunk - 1).bit_length() - 1)
    gr = jnp.swapaxes(gc.reshape(batch, n, chunk, DN_HEADS), 2, 3)
    buf8 = jnp.pad(dn_buf, ((0, 0), (SUBLANES - (DN_CONV - 1), 0), (0, 0)))
    kern = functools.partial(_delta_kernel, chunk=chunk, n_double=n_double)
    pcol = lambda c: pl.BlockSpec((1, chunk, DN_WIDTH), lambda b, i, c=c: (b, i, c // DN_WIDTH))
    bcol = lambda j: pl.BlockSpec((1, SUBLANES, DN_WIDTH), lambda b, i, j=j: (b, 0, j))
    wcol = lambda j: pl.BlockSpec((DN_CONV, DN_WIDTH), lambda b, i, j=j: (0, j))
    wide = pl.BlockSpec((1, chunk, DN_WIDTH), lambda b, i: (b, i, 0))
    narrow = pl.BlockSpec((1, chunk, DN_HEADS), lambda b, i: (b, i, 0))
    st = pl.BlockSpec((1, DN_HEADS, DN_HEAD_DIM, DN_HEAD_DIM), lambda b, i: (b, 0, 0, 0))
    return pl.pallas_call(
        kern,
        grid=(batch, n),
        in_specs=[pcol(COL_DQ), pcol(COL_DK), pcol(COL_DV), pcol(COL_DZ),
                  bcol(0), bcol(1), bcol(2), wcol(0), wcol(1), wcol(2),
                  narrow,
                  pl.BlockSpec((1, 1, DN_HEADS, chunk), lambda b, i: (b, i, 0, 0)),
                  narrow,
                  pl.BlockSpec((1, DN_HEAD_DIM), lambda b, i: (0, 0)),
                  st],
        out_specs=[wide, st],
        out_shape=[jax.ShapeDtypeStruct((batch, t, DN_WIDTH), BF16),
                   jax.ShapeDtypeStruct(state.shape, F32)],
        scratch_shapes=[pltpu.VMEM((chunk + SUBLANES, DN_WIDTH), F32)] * 3,
        compiler_params=_cparams(("parallel", "arbitrary")),
        name="delta_rule",
    )(proj, proj, proj, proj, buf8, buf8, buf8, conv_w, conv_w, conv_w,
      gc, gr, beta, gnorm.reshape(1, DN_HEAD_DIM), state)


def _layer(x, past_k, past_v, past_ik, dn_buf, dn_state, ffn_buf, lw, *, attn_tiles):
    (norm1_g, w_in, dn_conv_w, dn_a_log, dn_dt_bias, dn_norm_g, w_out, norm2_g, ffn_w_up, ffn_conv_w,
     ffn_w_down) = lw
    b, t, d = x.shape
    past = past_k.shape[1]
    n_keys = past + t
    x2 = x.reshape(b * t, d)

    proj = norm_matmul(x2, norm1_g, w_in)
    pos = jnp.tile(past + jnp.arange(t, dtype=I32), b)
    q_bf, iq_bf, k_f32, k_bf, v_bf, ik2_bf, small = attn_prep(proj, pos)
    proj3 = proj.reshape(b, t, IN_COLS)
    small = small.reshape(b, t, LANES)
    new_k = k_f32.reshape(b, t, N_KV_HEADS, HEAD_DIM)
    new_v = proj3[:, :, COL_AV:COL_AV + KV_WIDTH].reshape(b, t, N_KV_HEADS, HEAD_DIM)
    new_ik = small[:, :, SMALL_IK:SMALL_IK + IDX_HEAD_DIM]
    i_w = small[:, :, SMALL_IW:SMALL_IW + N_IDX_HEADS]

    tq, tk = attn_tiles
    lp = -(-n_keys // tk) * tk

    def with_past(old_bf, new_bf):
        allk = jnp.concatenate([old_bf, new_bf.reshape(b, t, -1)], axis=1)
        return jnp.pad(allk, ((0, 0), (0, lp - n_keys), (0, 0)))

    pik = past_ik.astype(BF16)
    zpad = jnp.zeros_like(pik)
    past_ik2 = jnp.concatenate([pik, zpad, zpad, pik], axis=-1)
    attn = dsa_attention(
        iq_bf.reshape(b, t, IDX_WIDTH), i_w, with_past(past_ik2, ik2_bf), q_bf.reshape(b, t, ATTN_WIDTH),
        with_past(past_k.reshape(b, past, KV_WIDTH).astype(BF16), k_bf),
        with_past(past_v.reshape(b, past, KV_WIDTH).astype(BF16), v_bf),
        n_keys=n_keys, past=past, tq=tq, tk=tk)

    new_dn_buf = jnp.concatenate([dn_buf, proj3[:, :, COL_DQ:COL_DQ + 3 * DN_WIDTH]], axis=1)[:, t:] if t < DN_CONV - 1 \
        else proj3[:, t - (DN_CONV - 1):, COL_DQ:COL_DQ + 3 * DN_WIDTH]
    beta = jax.nn.sigmoid(small[:, :, SMALL_DB:SMALL_DB + DN_HEADS])
    g = -jnp.exp(dn_a_log) * jax.nn.softplus(small[:, :, SMALL_DA:SMALL_DA + DN_HEADS] + dn_dt_bias)
    chunk = CHUNK if t % CHUNK == 0 else t
    gc = jnp.cumsum(g.reshape(b, t // chunk, chunk, DN_HEADS), axis=2).reshape(b, t, DN_HEADS)
    o_dn, new_state = delta_rule(proj3, dn_buf, dn_conv_w, gc, beta, dn_norm_g, dn_state, chunk)

    x2 = matmul_res([attn.reshape(b * t, ATTN_WIDTH), o_dn.reshape(b * t, DN_WIDTH)], w_out, x2)

    f2 = ffn_w_up.shape[1]
    xt = jnp.swapaxes(x2.reshape(b, t, d), 0, 1).reshape(t * b, d)
    halo = max(SUBLANES, (FFN_CONV - 1) * b)
    hist = jnp.swapaxes(ffn_buf, 0, 1).reshape((FFN_CONV - 1) * b, f2)
    hist = ffn_interleave(jnp.pad(hist, ((halo - hist.shape[0], 0), (0, 0))))
    act, tail = ffn_up(xt, norm2_g, ffn_w_up, ffn_conv_w, hist, step=b)
    tail = ffn_deinterleave(tail[halo - (FFN_CONV - 1) * b:])
    new_ffn_buf = jnp.swapaxes(tail.reshape(FFN_CONV - 1, b, f2), 0, 1)
    act = jnp.swapaxes(act.reshape(t, b, -1), 0, 1).reshape(b * t, -1)
    x2 = matmul_res([act], ffn_w_down, x2)
    return x2.reshape(b, t, d), new_k, new_v, new_ik, new_dn_buf, new_state, new_ffn_buf


def kernel(x_prompt, x_sample, cache_k, cache_v, cache_kidx, state_dn_conv, state_dn, state_ffn_conv,
           norm1_g, w_in, dn_conv_w, dn_a_log, dn_dt_bias, dn_norm_g, w_out, norm2_g,
           ffn_w_up, ffn_conv_w, ffn_w_down, final_g):
    depth = w_in.shape[0]
    bp, tp, d = x_prompt.shape
    bs, ts, _ = x_sample.shape
    f2 = ffn_w_up.shape[2]
    zk = jnp.zeros((bp, 0, N_KV_HEADS, HEAD_DIM), F32)
    zik = jnp.zeros((bp, 0, IDX_HEAD_DIM), F32)
    z_dn_buf = jnp.zeros((bp, DN_CONV - 1, 3 * DN_WIDTH), F32)
    z_dn_state = jnp.zeros((bp, DN_HEADS, DN_HEAD_DIM, DN_HEAD_DIM), F32)
    z_ffn_buf = jnp.zeros((bp, FFN_CONV - 1, f2), F32)

    tq_p = _pick(tp, (256, 128, 64, 32, 16))
    tk_p = _pick(tp, (512, 256, 128))
    n_keys_s = cache_k.shape[2] + ts
    tk_s = -(-n_keys_s // LANES) * LANES

    xp, xs = x_prompt, x_sample
    st_p, st_s = [], []
    for l in range(depth):
        lw = (norm1_g[l], _regroup_in_proj(w_in[l]), dn_conv_w[l], dn_a_log[l], dn_dt_bias[l],
              dn_norm_g[l], w_out[l].astype(BF16), norm2_g[l], ffn_w_up[l].astype(BF16),
              ffn_interleave(ffn_conv_w[l]), ffn_w_down[l].astype(BF16))
        xp, *sp = _layer(xp, zk, zk, zik, z_dn_buf, z_dn_state, z_ffn_buf, lw, attn_tiles=(tq_p, tk_p))
        xs, *ss = _layer(xs, cache_k[l], cache_v[l], cache_kidx[l], state_dn_conv[l], state_dn[l],
                         state_ffn_conv[l], lw, attn_tiles=(ts, tk_s))
        st_p.append(sp)
        st_s.append(ss)

    y_prompt = rms_norm_rows(xp.reshape(bp * tp, d), final_g).reshape(bp, tp, d)
    y_sample = rms_norm_rows(xs.reshape(bs * ts, d), final_g).reshape(bs, ts, d)

    def stacked(states, i):
        return jnp.stack([s[i] for s in states], axis=0)

    return (y_prompt, y_sample,
            *[stacked(st_p, i) for i in range(6)],
            *[stacked(st_s, i) for i in range(6)])
```

```python
import functools

import numpy as np
import jax
import jax.numpy as jnp
from jax import lax
from jax.experimental import pallas as pl
from jax.experimental.pallas import tpu as pltpu

F32 = jnp.float32
BF16 = jnp.bfloat16
I32 = jnp.int32

CHUNK = 64
CHUNK_SHIFT = CHUNK.bit_length() - 1
HEAD_DIM = 128
N_HEADS = 16
N_KV_HEADS = 4
KV_GROUP = N_HEADS // N_KV_HEADS
ATTN_WIDTH = N_HEADS * HEAD_DIM
KV_WIDTH = N_KV_HEADS * HEAD_DIM
N_IDX_HEADS = 32
IDX_HEAD_DIM = 64
IDX_WIDTH = N_IDX_HEADS * IDX_HEAD_DIM
IDX_SCALE = IDX_WIDTH ** -0.5
IDX_TOPK_MAX = 256
DN_HEADS = 16
DN_HEAD_DIM = 128
DN_WIDTH = DN_HEADS * DN_HEAD_DIM
DN_CONV = 4
FFN_CONV = 3
ROPE_THETA = 10000.0
EPS = 1e-6
LOG2E = 1.4426950408889634

LANES = 128
SUBLANES = 8
VMEM_LIMIT = 56 * 1024 * 1024

INT_MIN = -(2 ** 31)
INT_MAX = 2 ** 31 - 1
NEG_BIG = -1e30

COL_AQ, COL_IQ, COL_DQ, COL_DK, COL_DV, COL_DZ = (i * ATTN_WIDTH for i in range(6))
COL_AK = 6 * ATTN_WIDTH
COL_AV = COL_AK + KV_WIDTH
COL_SMALL = COL_AV + KV_WIDTH
SMALL_IK, SMALL_IW, SMALL_DB, SMALL_DA = 0, 64, 96, 112
IN_TILE = 768
IN_COLS = -(-(COL_SMALL + LANES) // IN_TILE) * IN_TILE
assert ATTN_WIDTH == IDX_WIDTH == DN_WIDTH


def _regroup_in_proj(w):
    splits = (ATTN_WIDTH, KV_WIDTH, KV_WIDTH, IDX_WIDTH, IDX_HEAD_DIM, N_IDX_HEADS,
              DN_WIDTH, DN_WIDTH, DN_WIDTH, DN_HEADS, DN_HEADS, DN_WIDTH)
    starts = np.concatenate([[0], np.cumsum(splits)])
    seg = {n: (int(starts[i]), int(starts[i + 1])) for i, n in enumerate(
        ("a_q", "a_k", "a_v", "i_q", "i_k", "i_w", "d_q", "d_k", "d_v", "d_b", "d_a", "d_z"))}
    order = ("a_q", "i_q", "d_q", "d_k", "d_v", "d_z", "a_k", "a_v", "i_k", "i_w", "d_b", "d_a")
    parts = [w[:, seg[n][0]:seg[n][1]].astype(BF16) for n in order]
    parts.append(jnp.zeros((w.shape[0], IN_COLS - int(starts[-1])), BF16))
    return jnp.concatenate(parts, axis=1)


def _cparams(sem):
    return pltpu.CompilerParams(dimension_semantics=sem, vmem_limit_bytes=VMEM_LIMIT)


def _pick(n, prefs):
    for p in prefs:
        if n % p == 0:
            return p
    raise ValueError(f"no tile in {prefs} divides {n}")


ROW_TILES = (512, 256, 128, 64, 32, 16, 8)


def _norm_matmul_kernel(x_ref, g_ref, w_ref, o_ref, h_ref):
    @pl.when(pl.program_id(1) == 0)
    def _():
        x = x_ref[...]
        ms = jnp.mean(x * x, axis=-1, keepdims=True)
        h_ref[...] = (x * lax.rsqrt(ms + EPS) * g_ref[...]).astype(BF16)

    o_ref[...] = jnp.dot(h_ref[...], w_ref[...], preferred_element_type=F32)


def norm_matmul(x, g, w):
    n, d = x.shape
    cols = w.shape[1]
    tm = _pick(n, ROW_TILES)
    tn = _pick(cols, (IN_TILE, 512, 256, 128))
    return pl.pallas_call(
        _norm_matmul_kernel,
        grid=(n // tm, cols // tn),
        in_specs=[pl.BlockSpec((tm, d), lambda i, j: (i, 0)),
                  pl.BlockSpec((1, d), lambda i, j: (0, 0)),
                  pl.BlockSpec((d, tn), lambda i, j: (0, j))],
        out_specs=pl.BlockSpec((tm, tn), lambda i, j: (i, j)),
        out_shape=jax.ShapeDtypeStruct((n, cols), F32),
        scratch_shapes=[pltpu.VMEM((tm, d), BF16)],
        compiler_params=_cparams(("parallel", "arbitrary")),
        name="norm_matmul",
    )(x, g.reshape(1, d), w)


def _matmul_res_kernel(*refs, n_lhs):
    a_refs, w_refs, x_ref, o_ref = refs[:n_lhs], refs[n_lhs:2 * n_lhs], refs[2 * n_lhs], refs[2 * n_lhs + 1]
    acc = x_ref[...]
    for a_ref, w_ref in zip(a_refs, w_refs):
        acc = acc + jnp.dot(a_ref[...], w_ref[...], preferred_element_type=F32)
    o_ref[...] = acc


def matmul_res(lhs, w, x):
    n, k = lhs[0].shape
    cols = w.shape[1]
    tm = _pick(n, ROW_TILES)
    tn = _pick(cols, (256,) if k > 8192 else (1024, 512, 256, 128))
    kern = functools.partial(_matmul_res_kernel, n_lhs=len(lhs))
    return pl.pallas_call(
        kern,
        grid=(n // tm, cols // tn),
        in_specs=([pl.BlockSpec((tm, k), lambda i, j: (i, 0)) for _ in lhs]
                  + [pl.BlockSpec((k, tn), lambda i, j, r=r: (r, j)) for r in range(len(lhs))]
                  + [pl.BlockSpec((tm, tn), lambda i, j: (i, j))]),
        out_specs=pl.BlockSpec((tm, tn), lambda i, j: (i, j)),
        out_shape=jax.ShapeDtypeStruct((n, cols), F32),
        compiler_params=_cparams(("parallel", "arbitrary")),
        name="matmul_res",
    )(*lhs, *([w] * len(lhs)), x)


def _rms_norm_kernel(x_ref, g_ref, o_ref):
    x = x_ref[...]
    ms = jnp.mean(x * x, axis=-1, keepdims=True)
    o_ref[...] = x * lax.rsqrt(ms + EPS) * g_ref[...]


def rms_norm_rows(x, g):
    n, d = x.shape
    tm = _pick(n, ROW_TILES)
    return pl.pallas_call(
        _rms_norm_kernel,
        grid=(n // tm,),
        in_specs=[pl.BlockSpec((tm, d), lambda i: (i, 0)), pl.BlockSpec((1, d), lambda i: (0, 0))],
        out_specs=pl.BlockSpec((tm, d), lambda i: (i, 0)),
        out_shape=jax.ShapeDtypeStruct((n, d), F32),
        compiler_params=_cparams(("parallel",)),
        name="rms_norm",
    )(x, g.reshape(1, d))


FFN_TILE = 256


def ffn_interleave(a):
    f = a.shape[-1] // 2
    lead = a.shape[:-1]
    return jnp.swapaxes(a.reshape(*lead, 2, f // FFN_TILE, FFN_TILE), -3, -2).reshape(*lead, 2 * f)


def ffn_deinterleave(a):
    f = a.shape[-1] // 2
    lead = a.shape[:-1]
    return jnp.swapaxes(a.reshape(*lead, f // FFN_TILE, 2, FFN_TILE), -3, -2).reshape(*lead, 2 * f)


def _ffn_up_kernel(x_ref, g_ref, wg_ref, wv_ref, c_ref, hist_ref, a_ref, st_ref, h_ref, e_ref, cy_ref,
                   *, tm, tn, step, halo):
    i = pl.program_id(0)
    j = pl.program_id(1)

    @pl.when(j == 0)
    def _():
        x = x_ref[...]
        ms = jnp.mean(x * x, axis=-1, keepdims=True)
        h_ref[...] = (x * lax.rsqrt(ms + EPS) * g_ref[...]).astype(BF16)

    h = h_ref[...]
    up = jnp.concatenate([jnp.dot(h, wg_ref[...], preferred_element_type=F32),
                          jnp.dot(h, wv_ref[...], preferred_element_type=F32)], axis=1)
    e_ref[halo:halo + tm, :] = up

    @pl.when(i == 0)
    def _():
        e_ref[0:halo, :] = hist_ref[...]

    @pl.when(i > 0)
    def _():
        e_ref[0:halo, :] = cy_ref[j]

    c = c_ref[...]
    y = (e_ref[halo - 2 * step:halo - 2 * step + tm, :] * c[0:1, :]
         + e_ref[halo - step:halo - step + tm, :] * c[1:2, :]
         + up * c[2:3, :])
    tail = e_ref[tm:tm + halo, :]
    cy_ref[j] = tail
    st_ref[...] = tail
    yg = y[:, :tn]
    a_ref[...] = (yg * jax.nn.sigmoid(yg) * y[:, tn:]).astype(BF16)


def ffn_up(x, g, w_up, conv_w, hist, step):
    n, d = x.shape
    f2 = w_up.shape[1]
    f = f2 // 2
    halo = hist.shape[0]
    tm = _pick(n, ROW_TILES)
    tn = FFN_TILE
    nj = f // tn
    kern = functools.partial(_ffn_up_kernel, tm=tm, tn=tn, step=step, halo=halo)
    act, tail = pl.pallas_call(
        kern,
        grid=(n // tm, nj),
        in_specs=[pl.BlockSpec((tm, d), lambda i, j: (i, 0)),
                  pl.BlockSpec((1, d), lambda i, j: (0, 0)),
                  pl.BlockSpec((d, tn), lambda i, j: (0, j)),
                  pl.BlockSpec((d, tn), lambda i, j: (0, nj + j)),
                  pl.BlockSpec((FFN_CONV, 2 * tn), lambda i, j: (0, j)),
                  pl.BlockSpec((halo, 2 * tn), lambda i, j: (0, j))],
        out_specs=[pl.BlockSpec((tm, tn), lambda i, j: (i, j)),
                   pl.BlockSpec((None, halo, 2 * tn), lambda i, j: (i, 0, j))],
        out_shape=[jax.ShapeDtypeStruct((n, f), BF16),
                   jax.ShapeDtypeStruct((n // tm, halo, f2), F32)],
        scratch_shapes=[pltpu.VMEM((tm, d), BF16),
                        pltpu.VMEM((tm + halo, 2 * tn), F32),
                        pltpu.VMEM((nj, halo, 2 * tn), F32)],
        compiler_params=_cparams(("arbitrary", "arbitrary")),
        name="ffn_up",
    )(x, g.reshape(1, d), w_up, w_up, conv_w, hist)
    return act, tail[-1]


def _attn_prep_kernel(aq_ref, iq_ref, ak_ref, av_ref, sm_ref, cosa_ref, sina_ref, cosi_ref, sini_ref,
                      q_ref, iqo_ref, kf_ref, kb_ref, vb_ref, ik2_ref, smo_ref, *, tm):
    cosa, sina = cosa_ref[...], sina_ref[...]
    cosi, sini = cosi_ref[...], sini_ref[...]
    lane = lax.broadcasted_iota(I32, (tm, LANES), 1)
    first_half = (lane & (IDX_HEAD_DIM - 1)) < IDX_HEAD_DIM // 2

    def rot_head(x):
        return x * cosa + pltpu.roll(x, HEAD_DIM // 2, 1) * sina

    def rot_idx(x):
        partner = jnp.where(first_half, pltpu.roll(x, LANES - IDX_HEAD_DIM // 2, 1), pltpu.roll(x, IDX_HEAD_DIM // 2, 1))
        return x * cosi + partner * sini

    qscale = (HEAD_DIM ** -0.5) * LOG2E
    for h in range(N_HEADS):
        sl = slice(h * LANES, (h + 1) * LANES)
        q_ref[:, sl] = (rot_head(aq_ref[:, sl]) * qscale).astype(BF16)
        iqo_ref[:, sl] = rot_idx(iq_ref[:, sl]).astype(BF16)
    for n in range(N_KV_HEADS):
        sl = slice(n * LANES, (n + 1) * LANES)
        kr = rot_head(ak_ref[:, sl])
        kf_ref[:, sl] = kr
        kb_ref[:, sl] = kr.astype(BF16)
    vb_ref[...] = av_ref[...].astype(BF16)
    sm = sm_ref[...]
    ik = jnp.where(lane < SMALL_IW, rot_idx(sm), 0.0)
    smo_ref[...] = jnp.where(lane < SMALL_IW, ik, jnp.where(lane < SMALL_DB, sm * IDX_SCALE, sm))
    ik2_ref[:, :LANES] = ik.astype(BF16)
    ik2_ref[:, LANES:] = pltpu.roll(ik, IDX_HEAD_DIM, 1).astype(BF16)


def attn_prep(proj, pos):
    n = proj.shape[0]
    tm = _pick(n, (256, 128, 64, 32, 16, 8))
    posf = pos.astype(F32)[:, None]

    def tables(half):
        inv_freq = jnp.float32(ROPE_THETA) ** (-jnp.arange(half, dtype=F32) / half)
        ang = posf * inv_freq[None, :]
        cos, sin = jnp.cos(ang), jnp.sin(ang)
        rep = LANES // (2 * half)
        return jnp.tile(jnp.concatenate([cos, cos], axis=1), (1, rep)), jnp.tile(jnp.concatenate([-sin, sin], axis=1), (1, rep))

    cosa, sina = tables(HEAD_DIM // 2)
    cosi, sini = tables(IDX_HEAD_DIM // 2)
    wide = lambda c: pl.BlockSpec((tm, ATTN_WIDTH), lambda i, c=c: (i, c // ATTN_WIDTH))
    kvw = lambda c: pl.BlockSpec((tm, KV_WIDTH), lambda i, c=c: (i, c // KV_WIDTH))
    tab = pl.BlockSpec((tm, LANES), lambda i: (i, 0))
    row = lambda w: pl.BlockSpec((tm, w), lambda i: (i, 0))
    return pl.pallas_call(
        functools.partial(_attn_prep_kernel, tm=tm),
        grid=(n // tm,),
        in_specs=[wide(COL_AQ), wide(COL_IQ), kvw(COL_AK), kvw(COL_AV),
                  pl.BlockSpec((tm, LANES), lambda i: (i, COL_SMALL // LANES)), tab, tab, tab, tab],
        out_specs=[row(ATTN_WIDTH), row(IDX_WIDTH), row(KV_WIDTH), row(KV_WIDTH), row(KV_WIDTH), row(2 * LANES), row(LANES)],
        out_shape=[jax.ShapeDtypeStruct((n, ATTN_WIDTH), BF16),
                   jax.ShapeDtypeStruct((n, IDX_WIDTH), BF16),
                   jax.ShapeDtypeStruct((n, KV_WIDTH), F32),
                   jax.ShapeDtypeStruct((n, KV_WIDTH), BF16),
                   jax.ShapeDtypeStruct((n, KV_WIDTH), BF16),
                   jax.ShapeDtypeStruct((n, 2 * LANES), BF16),
                   jax.ShapeDtypeStruct((n, LANES), F32)],
        compiler_params=_cparams(("parallel",)),
        name="attn_prep",
    )(proj, proj, proj, proj, proj, cosa, sina, cosi, sini)


IDX_PAIR_GROUP = 8
N_IDX_PAIRS = N_IDX_HEADS // 2


def _dsa_kernel(tb_ref, tqb_ref, tph_ref, tkb_ref, tnkb_ref, tikb_ref, tkvb_ref,
                iq_ref, iw_ref, ik_ref, q_ref, k_ref, v_ref, o_ref,
                keys_ref, sc_ref, kmax_ref, k2nd_ref, thr_ref, pcut_ref, wb_ref, iqs_ref, qs_ref, m_ref, acc_ref,
                *, tq, tk, n_keys, past, topk, pos_bits):
    s = pl.program_id(0)
    ph = tph_ref[s]
    kb = tkb_ref[s]
    nkb = tnkb_ref[s]
    q0 = tqb_ref[s] * tq + past
    ncol = tk // LANES
    grows = IDX_PAIR_GROUP * tq

    @pl.when(ph == 0)
    def _scores():
        @pl.when(kb == 0)
        def _():
            w = iw_ref[0]
            for h in range(N_IDX_HEADS):
                wb_ref[h] = jnp.broadcast_to(w[:, h:h + 1], (tq, LANES))
            for p in range(N_IDX_PAIRS):
                iqs_ref[p * tq:(p + 1) * tq, :] = iq_ref[0, :, p * LANES:(p + 1) * LANES]

        ik2 = ik_ref[0]
        ik_even, ik_odd = ik2[:, :LANES], ik2[:, LANES:]
        sc_ref[...] = jnp.zeros((tq, tk), F32)

        for gi in range(N_IDX_PAIRS // IDX_PAIR_GROUP):
            lhs = iqs_ref[gi * grows:(gi + 1) * grows, :]
            raws = [lax.dot_general(lhs, kk, (((1,), (1,)), ((), ())), preferred_element_type=F32)
                    for kk in (ik_even, ik_odd)]
            part = None
            for j in range(IDX_PAIR_GROUP):
                for par in range(2):
                    wv = wb_ref[(gi * IDX_PAIR_GROUP + j) * 2 + par]
                    wv = jnp.concatenate([wv] * ncol, axis=1) if ncol > 1 else wv
                    term = wv * jnp.maximum(raws[par][j * tq:(j + 1) * tq, :], 0.0)
                    part = term if part is None else part + term
            sc_ref[...] += part

        sc = sc_ref[...] + 0.0
        bits = pltpu.bitcast(sc, I32)
        key = bits ^ ((bits >> 31) & INT_MAX)
        kpos = kb * tk + lax.broadcasted_iota(I32, (tq, tk), 1)
        qpos = q0 + lax.broadcasted_iota(I32, (tq, tk), 0)
        adm = jnp.logical_and((kpos >> CHUNK_SHIFT) <= (qpos >> CHUNK_SHIFT), kpos < n_keys)
        key = jnp.where(adm, key, INT_MIN)
        keys_ref[kb] = key
        k1 = key[:, :LANES]
        k2 = jnp.full((tq, LANES), INT_MIN, I32)
        for c in range(1, ncol):
            x = key[:, c * LANES:(c + 1) * LANES]
            k2 = jnp.maximum(k2, jnp.minimum(k1, x))
            k1 = jnp.maximum(k1, x)

        @pl.when(kb == 0)
        def _():
            kmax_ref[...] = k1
            k2nd_ref[...] = k2

        @pl.when(kb > 0)
        def _():
            o1 = kmax_ref[...]
            kmax_ref[...] = jnp.maximum(o1, k1)
            k2nd_ref[...] = jnp.maximum(jnp.minimum(o1, k1), jnp.maximum(k2nd_ref[...], k2))

    @pl.when(ph == 1)
    def _attend():
        def count_rows(pred):
            def blk(b, cnt):
                hit = pred(keys_ref[b], b).astype(I32)
                for c in range(ncol):
                    cnt = cnt + hit[:, c * LANES:(c + 1) * LANES]
                return cnt

            cnt = lax.fori_loop(0, nkb, blk, jnp.zeros((tq, LANES), I32))
            return jnp.sum(cnt, axis=1, keepdims=True)

        @pl.when(kb == 0)
        def _select():
            qrow = q0 + lax.broadcasted_iota(I32, (tq, 1), 0)
            n_adm = jnp.minimum(((qrow >> CHUNK_SHIFT) + 1) << CHUNK_SHIFT, n_keys)
            keep_all = n_adm <= topk
            lo0 = jnp.where(keep_all, INT_MIN + 1,
                            jnp.maximum(jnp.min(k2nd_ref[...], axis=1, keepdims=True), INT_MIN + 1))
            hi0 = jnp.max(kmax_ref[...], axis=1, keepdims=True) + 1
            cnt0 = count_rows(lambda x, b: x >= lo0)
            done0 = jnp.logical_or(keep_all, cnt0 == topk).astype(I32)

            def bis_cond(c):
                return jnp.logical_and(c[0] < 34, c[5] > 0)

            def bis_body(c):
                it, lo, hi, cnt_lo, done, _ = c
                mid = (lo >> 1) + (hi >> 1) + (lo & hi & 1)
                tot = count_rows(lambda x, b: x >= mid)
                move = jnp.logical_and(tot >= topk, done == 0)
                shrink = jnp.logical_and(tot < topk, done == 0)
                cnt_new = jnp.where(move, tot, cnt_lo)
                done = jnp.where(jnp.logical_or(cnt_new == topk, mid == lo), 1, done)
                return (it + 1, jnp.where(move, mid, lo), jnp.where(shrink, mid, hi), cnt_new, done,
                        jnp.sum(1 - done))

            _, thr, _, cnt_thr, _, _ = lax.while_loop(
                bis_cond, bis_body, (jnp.int32(0), lo0, hi0, cnt0, done0, jnp.sum(1 - done0)))
            thr_ref[...] = thr
            pcut_ref[...] = jnp.full((tq, 1), INT_MAX, I32)

            tied = cnt_thr > topk

            @pl.when(jnp.max(tied.astype(I32)) > 0)
            def _ties():
                room = topk - count_rows(lambda x, b: x > thr)

                def pos_body(it, cut):
                    cand = cut + lax.shift_left(jnp.int32(1), pos_bits - 1 - it)

                    def pred(x, b):
                        kpos = b * tk + lax.broadcasted_iota(I32, (tq, tk), 1)
                        return jnp.logical_and(x == thr, kpos < cand)

                    return jnp.where(count_rows(pred) < room, cand, cut)

                cut = lax.fori_loop(0, pos_bits, pos_body, jnp.zeros((tq, 1), I32))
                pcut_ref[...] = jnp.where(tied, cut, INT_MAX)

            m_ref[...] = jnp.full(m_ref.shape, NEG_BIG, F32)
            acc_ref[...] = jnp.zeros(acc_ref.shape, F32)
            qv = q_ref[0]
            for n in range(N_KV_HEADS):
                for g in range(KV_GROUP):
                    hd = n * KV_GROUP + g
                    qs_ref[n, g * tq:(g + 1) * tq, :] = qv[:, hd * HEAD_DIM:(hd + 1) * HEAD_DIM]

        key = keys_ref[kb]
        thr = thr_ref[...]
        kpos = kb * tk + lax.broadcasted_iota(I32, (tq, tk), 1)
        keep = jnp.logical_or(key > thr, jnp.logical_and(key == thr, kpos <= pcut_ref[...]))
        bias = jnp.where(keep, 0.0, NEG_BIG).astype(F32)
        bias4 = jnp.concatenate([bias] * KV_GROUP, axis=0)
        kv = k_ref[0]
        vv = v_ref[0]
        ones = jnp.ones((tk, HEAD_DIM), BF16)
        ns = range(N_KV_HEADS)
        lgs = [lax.dot_general(qs_ref[n], kv[:, n * HEAD_DIM:(n + 1) * HEAD_DIM], (((1,), (1,)), ((), ())),
                               preferred_element_type=F32) + bias4 for n in ns]
        ps, alphas = [], []
        for n in ns:
            cols = [lgs[n][:, c * LANES:(c + 1) * LANES] for c in range(ncol)]
            mx = cols[0]
            for c in range(1, ncol):
                mx = jnp.maximum(mx, cols[c])
            m_prev = m_ref[n]
            m_new = jnp.maximum(m_prev, jnp.max(mx, axis=1, keepdims=True))
            alphas.append(jnp.exp2(m_prev - m_new))
            ps.append(jnp.concatenate([jnp.exp2(cols[c] - m_new).astype(BF16) for c in range(ncol)], axis=1))
            m_ref[n] = m_new
        for n in ns:
            ve = jnp.concatenate([vv[:, n * HEAD_DIM:(n + 1) * HEAD_DIM], ones], axis=1)
            pv = jnp.dot(ps[n], ve, preferred_element_type=F32)
            acc_ref[n] = jnp.concatenate([alphas[n], alphas[n]], axis=1) * acc_ref[n] + pv

        @pl.when(kb == nkb - 1)
        def _finish():
            for n in range(N_KV_HEADS):
                a = acc_ref[n]
                o = a[:, :HEAD_DIM] / a[:, HEAD_DIM:]
                for g in range(KV_GROUP):
                    hd = n * KV_GROUP + g
                    o_ref[0, :, hd * HEAD_DIM:(hd + 1) * HEAD_DIM] = o[g * tq:(g + 1) * tq, :].astype(BF16)


def _dsa_tables(batch, t, tq, tk, past, n_keys):
    rows = []
    for b in range(batch):
        for qb in range(t // tq):
            last_q = past + qb * tq + tq - 1
            limit = min(n_keys, (last_q // CHUNK + 1) * CHUNK)
            nkb = -(-limit // tk)
            for ph in range(2):
                for kb in range(nkb):
                    ikb = kb if ph == 0 else nkb - 1
                    kvb = 0 if ph == 0 else kb
                    rows.append((b, qb, ph, kb, nkb, ikb, kvb))
    tab = np.asarray(rows, dtype=np.int32).T
    return [jnp.asarray(tab[r]) for r in range(tab.shape[0])]


def dsa_attention(iq, iw, ik2, q, k, v, *, n_keys, past, tq, tk):
    batch, t, _ = iq.shape
    lp = ik2.shape[1]
    assert t % tq == 0 and lp % tk == 0 and tk % LANES == 0
    topk = min(IDX_TOPK_MAX, n_keys // 4)
    assert topk <= 2 * LANES
    tables = _dsa_tables(batch, t, tq, tk, past, n_keys)
    nsteps = int(tables[0].shape[0])
    nkb_max = lp // tk
    kern = functools.partial(_dsa_kernel, tq=tq, tk=tk, n_keys=n_keys, past=past, topk=topk,
                             pos_bits=max(1, (lp - 1).bit_length()))
    qmap = lambda s, tb, tqb, tph, tkb, tnkb, tikb, tkvb: (tb[s], tqb[s], 0)
    grid_spec = pltpu.PrefetchScalarGridSpec(
        num_scalar_prefetch=7,
        grid=(nsteps,),
        in_specs=[
            pl.BlockSpec((1, tq, IDX_WIDTH), qmap),
            pl.BlockSpec((1, tq, N_IDX_HEADS), qmap),
            pl.BlockSpec((1, tk, 2 * LANES), lambda s, tb, tqb, tph, tkb, tnkb, tikb, tkvb: (tb[s], tikb[s], 0)),
            pl.BlockSpec((1, tq, ATTN_WIDTH), qmap),
            pl.BlockSpec((1, tk, KV_WIDTH), lambda s, tb, tqb, tph, tkb, tnkb, tikb, tkvb: (tb[s], tkvb[s], 0)),
            pl.BlockSpec((1, tk, KV_WIDTH), lambda s, tb, tqb, tph, tkb, tnkb, tikb, tkvb: (tb[s], tkvb[s], 0)),
        ],
        out_specs=pl.BlockSpec((1, tq, ATTN_WIDTH), qmap),
        scratch_shapes=[
            pltpu.VMEM((nkb_max, tq, tk), I32),
            pltpu.VMEM((tq, tk), F32),
            pltpu.VMEM((tq, LANES), I32),
            pltpu.VMEM((tq, LANES), I32),
            pltpu.VMEM((tq, 1), I32),
            pltpu.VMEM((tq, 1), I32),
            pltpu.VMEM((N_IDX_HEADS, tq, LANES), F32),
            pltpu.VMEM((N_IDX_PAIRS * tq, LANES), BF16),
            pltpu.VMEM((N_KV_HEADS, KV_GROUP * tq, HEAD_DIM), BF16),
            pltpu.VMEM((N_KV_HEADS, KV_GROUP * tq, LANES), F32),
            pltpu.VMEM((N_KV_HEADS, KV_GROUP * tq, 2 * HEAD_DIM), F32),
        ],
    )
    return pl.pallas_call(
        kern,
        grid_spec=grid_spec,
        out_shape=jax.ShapeDtypeStruct((batch, t, ATTN_WIDTH), BF16),
        compiler_params=_cparams(("arbitrary",)),
        name="dsa_attention",
    )(*tables, iq, iw, ik2, q, k, v)


def _mm(a, b):
    return jnp.dot(a.astype(BF16), b.astype(BF16), preferred_element_type=F32)


def _mm_nt(a, b):
    return lax.dot_general(a.astype(BF16), b.astype(BF16), (((1,), (1,)), ((), ())), preferred_element_type=F32)


def _mm_tn(a, b):
    return lax.dot_general(a.astype(BF16), b.astype(BF16), (((0,), (0,)), ((), ())), preferred_element_type=F32)


def _delta_kernel(dq_ref, dk_ref, dv_ref, z_ref, bq_ref, bk_ref, bv_ref, cq_ref, ck_ref, cv_ref,
                  gc_ref, gr_ref, b_ref, gn_ref, s0_ref, o_ref, s_ref, eq_ref, ek_ref, ev_ref,
                  *, chunk, n_double):
    first = pl.program_id(1) == 0

    @pl.when(first)
    def _():
        s_ref[...] = s0_ref[...]
        for e_ref, buf_ref in ((eq_ref, bq_ref), (ek_ref, bk_ref), (ev_ref, bv_ref)):
            e_ref[0:SUBLANES, :] = buf_ref[0]

    def conv_silu(x_ref, cw_ref, e_ref):
        e_ref[SUBLANES:SUBLANES + chunk, :] = x_ref[0]
        w = cw_ref[...]
        y = e_ref[SUBLANES:SUBLANES + chunk, :] * w[DN_CONV - 1:DN_CONV, :]
        for i in range(DN_CONV - 1):
            off = SUBLANES - (DN_CONV - 1) + i
            y = y + e_ref[off:off + chunk, :] * w[i:i + 1, :]
        e_ref[0:SUBLANES, :] = e_ref[chunk:chunk + SUBLANES, :]
        return y * jax.nn.sigmoid(y)

    yq = conv_silu(dq_ref, cq_ref, eq_ref)
    yk = conv_silu(dk_ref, ck_ref, ek_ref)
    yv = conv_silu(dv_ref, cv_ref, ev_ref)

    row = lax.broadcasted_iota(I32, (chunk, chunk), 0)
    col = lax.broadcasted_iota(I32, (chunk, chunk), 1)
    eye = (row == col).astype(F32)
    gcs = gc_ref[0]
    grs = gr_ref[0, 0]
    betas = b_ref[0]
    gn = gn_ref[...]
    hs = range(DN_HEADS)
    sl = [slice(h * DN_HEAD_DIM, (h + 1) * DN_HEAD_DIM) for h in hs]

    def l2n(x):
        return x * lax.rsqrt(jnp.sum(x * x, axis=-1, keepdims=True) + EPS)

    kh = [l2n(yk[:, sl[h]]) for h in hs]
    qh = [l2n(yq[:, sl[h]]) * (DN_HEAD_DIM ** -0.5) for h in hs]
    gc = [gcs[:, h:h + 1] for h in hs]
    beta = [betas[:, h:h + 1] for h in hs]
    decay = [jnp.exp(jnp.where(row >= col, gc[h] - grs[h:h + 1, :], -jnp.inf)) for h in hs]
    kk = [_mm_nt(kh[h], kh[h]) for h in hs]
    qk = [_mm_nt(qh[h], kh[h]) * decay[h] for h in hs]
    x = [-jnp.where(row > col, beta[h] * kk[h] * decay[h], 0.0) for h in hs]
    parts = [x]
    for _ in range(n_double):
        x = [_mm(x[h], x[h]) for h in hs]
        parts.append(x)
    while len(parts) > 1:
        nxt = [[parts[i][h] + parts[i + 1][h] + _mm(parts[i][h], parts[i + 1][h]) for h in hs]
               for i in range(0, len(parts) - 1, 2)]
        if len(parts) % 2:
            nxt.append(parts[-1])
        parts = nxt
    t = [eye + parts[0][h] for h in hs]
    eg = [jnp.exp(gc[h]) for h in hs]
    uw = [_mm(t[h], jnp.concatenate([yv[:, sl[h]] * beta[h], kh[h] * (beta[h] * eg[h])], axis=1)) for h in hs]
    s = [s_ref[0, h] for h in hs]
    ws_qs = [_mm(jnp.concatenate([uw[h][:, DN_HEAD_DIM:], qh[h] * eg[h]], axis=0), s[h]) for h in hs]
    v_new = [uw[h][:, :DN_HEAD_DIM] - ws_qs[h][:chunk] for h in hs]
    gl = [gc[h][chunk - 1:chunk, :] for h in hs]
    o = [ws_qs[h][chunk:] + _mm(qk[h], v_new[h]) for h in hs]
    upd = [_mm_tn(kh[h] * jnp.exp(gl[h] - gc[h]), v_new[h]) for h in hs]
    for h in hs:
        s_ref[0, h] = s[h] * jnp.exp(gl[h]) + upd[h]
        ms = jnp.mean(o[h] * o[h], axis=-1, keepdims=True)
        on = o[h] * lax.rsqrt(ms + EPS) * gn
        z = z_ref[0, :, sl[h]]
        o_ref[0, :, sl[h]] = (on * (z * jax.nn.sigmoid(z))).astype(BF16)


def delta_rule(proj, dn_buf, conv_w, gc, beta, gnorm, state, chunk):
    batch, t, _ = proj.shape
    n = t // chunk
    n_double = max(0, (chunk - 1).bit_length() - 1)
    gr = jnp.swapaxes(gc.reshape(batch, n, chunk, DN_HEADS), 2, 3)
    buf8 = jnp.pad(dn_buf, ((0, 0), (SUBLANES - (DN_CONV - 1), 0), (0, 0)))
    kern = functools.partial(_delta_kernel, chunk=chunk, n_double=n_double)
    pcol = lambda c: pl.BlockSpec((1, chunk, DN_WIDTH), lambda b, i, c=c: (b, i, c // DN_WIDTH))
    bcol = lambda j: pl.BlockSpec((1, SUBLANES, DN_WIDTH), lambda b, i, j=j: (b, 0, j))
    wcol = lambda j: pl.BlockSpec((DN_CONV, DN_WIDTH), lambda b, i, j=j: (0, j))
    wide = pl.BlockSpec((1, chunk, DN_WIDTH), lambda b, i: (b, i, 0))
    narrow = pl.BlockSpec((1, chunk, DN_HEADS), lambda b, i: (b, i, 0))
    st = pl.BlockSpec((1, DN_HEADS, DN_HEAD_DIM, DN_HEAD_DIM), lambda b, i: (b, 0, 0, 0))
    return pl.pallas_call(
        kern,
        grid=(batch, n),
        in_specs=[pcol(COL_DQ), pcol(COL_DK), pcol(COL_DV), pcol(COL_DZ),
                  bcol(0), bcol(1), bcol(2), wcol(0), wcol(1), wcol(2),
                  narrow,
                  pl.BlockSpec((1, 1, DN_HEADS, chunk), lambda b, i: (b, i, 0, 0)),
                  narrow,
                  pl.BlockSpec((1, DN_HEAD_DIM), lambda b, i: (0, 0)),
                  st],
        out_specs=[wide, st],
        out_shape=[jax.ShapeDtypeStruct((batch, t, DN_WIDTH), BF16),
                   jax.ShapeDtypeStruct(state.shape, F32)],
        scratch_shapes=[pltpu.VMEM((chunk + SUBLANES, DN_WIDTH), F32)] * 3,
        compiler_params=_cparams(("parallel", "arbitrary")),
        name="delta_rule",
    )(proj, proj, proj, proj, buf8, buf8, buf8, conv_w, conv_w, conv_w,
      gc, gr, beta, gnorm.reshape(1, DN_HEAD_DIM), state)


def _layer(x, past_k, past_v, past_ik, dn_buf, dn_state, ffn_buf, lw, *, attn_tiles):
    (norm1_g, w_in, dn_conv_w, dn_a_log, dn_dt_bias, dn_norm_g, w_out, norm2_g, ffn_w_up, ffn_conv_w,
     ffn_w_down) = lw
    b, t, d = x.shape
    past = past_k.shape[1]
    n_keys = past + t
    x2 = x.reshape(b * t, d)

    proj = norm_matmul(x2, norm1_g, w_in)
    pos = jnp.tile(past + jnp.arange(t, dtype=I32), b)
    q_bf, iq_bf, k_f32, k_bf, v_bf, ik2_bf, small = attn_prep(proj, pos)
    proj3 = proj.reshape(b, t, IN_COLS)
    small = small.reshape(b, t, LANES)
    new_k = k_f32.reshape(b, t, N_KV_HEADS, HEAD_DIM)
    new_v = proj3[:, :, COL_AV:COL_AV + KV_WIDTH].reshape(b, t, N_KV_HEADS, HEAD_DIM)
    new_ik = small[:, :, SMALL_IK:SMALL_IK + IDX_HEAD_DIM]
    i_w = small[:, :, SMALL_IW:SMALL_IW + N_IDX_HEADS]

    tq, tk = attn_tiles
    lp = -(-n_keys // tk) * tk

    def with_past(old_bf, new_bf):
        allk = jnp.concatenate([old_bf, new_bf.reshape(b, t, -1)], axis=1)
        return jnp.pad(allk, ((0, 0), (0, lp - n_keys), (0, 0)))

    pik = past_ik.astype(BF16)
    zpad = jnp.zeros_like(pik)
    past_ik2 = jnp.concatenate([pik, zpad, zpad, pik], axis=-1)
    attn = dsa_attention(
        iq_bf.reshape(b, t, IDX_WIDTH), i_w, with_past(past_ik2, ik2_bf), q_bf.reshape(b, t, ATTN_WIDTH),
        with_past(past_k.reshape(b, past, KV_WIDTH).astype(BF16), k_bf),
        with_past(past_v.reshape(b, past, KV_WIDTH).astype(BF16), v_bf),
        n_keys=n_keys, past=past, tq=tq, tk=tk)

    new_dn_buf = jnp.concatenate([dn_buf, proj3[:, :, COL_DQ:COL_DQ + 3 * DN_WIDTH]], axis=1)[:, t:] if t < DN_CONV - 1 \
        else proj3[:, t - (DN_CONV - 1):, COL_DQ:COL_DQ + 3 * DN_WIDTH]
    beta = jax.nn.sigmoid(small[:, :, SMALL_DB:SMALL_DB + DN_HEADS])
    g = -jnp.exp(dn_a_log) * jax.nn.softplus(small[:, :, SMALL_DA:SMALL_DA + DN_HEADS] + dn_dt_bias)
    chunk = CHUNK if t % CHUNK == 0 else t
    gc = jnp.cumsum(g.reshape(b, t // chunk, chunk, DN_HEADS), axis=2).reshape(b, t, DN_HEADS)
    o_dn, new_state = delta_rule(proj3, dn_buf, dn_conv_w, gc, beta, dn_norm_g, dn_state, chunk)

    x2 = matmul_res([attn.reshape(b * t, ATTN_WIDTH), o_dn.reshape(b * t, DN_WIDTH)], w_out, x2)

    f2 = ffn_w_up.shape[1]
    xt = jnp.swapaxes(x2.reshape(b, t, d), 0, 1).reshape(t * b, d)
    halo = max(SUBLANES, (FFN_CONV - 1) * b)
    hist = jnp.swapaxes(ffn_buf, 0, 1).reshape((FFN_CONV - 1) * b, f2)
    hist = ffn_interleave(jnp.pad(hist, ((halo - hist.shape[0], 0), (0, 0))))
    act, tail = ffn_up(xt, norm2_g, ffn_w_up, ffn_conv_w, hist, step=b)
    tail = ffn_deinterleave(tail[halo - (FFN_CONV - 1) * b:])
    new_ffn_buf = jnp.swapaxes(tail.reshape(FFN_CONV - 1, b, f2), 0, 1)
    act = jnp.swapaxes(act.reshape(t, b, -1), 0, 1).reshape(b * t, -1)
    x2 = matmul_res([act], ffn_w_down, x2)
    return x2.reshape(b, t, d), new_k, new_v, new_ik, new_dn_buf, new_state, new_ffn_buf


def kernel(x_prompt, x_sample, cache_k, cache_v, cache_kidx, state_dn_conv, state_dn, state_ffn_conv,
           norm1_g, w_in, dn_conv_w, dn_a_log, dn_dt_bias, dn_norm_g, w_out, norm2_g,
           ffn_w_up, ffn_conv_w, ffn_w_down, final_g):
    depth = w_in.shape[0]
    bp, tp, d = x_prompt.shape
    bs, ts, _ = x_sample.shape
    f2 = ffn_w_up.shape[2]
    zk = jnp.zeros((bp, 0, N_KV_HEADS, HEAD_DIM), F32)
    zik = jnp.zeros((bp, 0, IDX_HEAD_DIM), F32)
    z_dn_buf = jnp.zeros((bp, DN_CONV - 1, 3 * DN_WIDTH), F32)
    z_dn_state = jnp.zeros((bp, DN_HEADS, DN_HEAD_DIM, DN_HEAD_DIM), F32)
    z_ffn_buf = jnp.zeros((bp, FFN_CONV - 1, f2), F32)

    tq_p = _pick(tp, (256, 128, 64, 32, 16))
    tk_p = _pick(tp, (512, 256, 128))
    n_keys_s = cache_k.shape[2] + ts
    tk_s = -(-n_keys_s // LANES) * LANES

    xp, xs = x_prompt, x_sample
    st_p, st_s = [], []
    for l in range(depth):
        lw = (norm1_g[l], _regroup_in_proj(w_in[l]), dn_conv_w[l], dn_a_log[l], dn_dt_bias[l],
              dn_norm_g[l], w_out[l].astype(BF16), norm2_g[l], ffn_w_up[l].astype(BF16),
              ffn_interleave(ffn_conv_w[l]), ffn_w_down[l].astype(BF16))
        xp, *sp = _layer(xp, zk, zk, zik, z_dn_buf, z_dn_state, z_ffn_buf, lw, attn_tiles=(tq_p, tk_p))
        xs, *ss = _layer(xs, cache_k[l], cache_v[l], cache_kidx[l], state_dn_conv[l], state_dn[l],
                         state_ffn_conv[l], lw, attn_tiles=(ts, tk_s))
        st_p.append(sp)
        st_s.append(ss)

    y_prompt = rms_norm_rows(xp.reshape(bp * tp, d), final_g).reshape(bp, tp, d)
    y_sample = rms_norm_rows(xs.reshape(bs * ts, d), final_g).reshape(bs, ts, d)

    def stacked(states, i):
        return jnp.stack([s[i] for s in states], axis=0)

    return (y_prompt, y_sample,
            *[stacked(st_p, i) for i in range(6)],
            *[stacked(st_s, i) for i in range(6)])
```

```python
import functools

import numpy as np
import jax
import jax.numpy as jnp
from jax import lax
from jax.experimental import pallas as pl
from jax.experimental.pallas import tpu as pltpu

F32 = jnp.float32
BF16 = jnp.bfloat16
I32 = jnp.int32

CHUNK = 64
CHUNK_SHIFT = CHUNK.bit_length() - 1
HEAD_DIM = 128
N_HEADS = 16
N_KV_HEADS = 4
KV_GROUP = N_HEADS // N_KV_HEADS
ATTN_WIDTH = N_HEADS * HEAD_DIM
KV_WIDTH = N_KV_HEADS * HEAD_DIM
N_IDX_HEADS = 32
IDX_HEAD_DIM = 64
IDX_WIDTH = N_IDX_HEADS * IDX_HEAD_DIM
IDX_SCALE = IDX_WIDTH ** -0.5
IDX_TOPK_MAX = 256
DN_HEADS = 16
DN_HEAD_DIM = 128
DN_WIDTH = DN_HEADS * DN_HEAD_DIM
DN_CONV = 4
FFN_CONV = 3
ROPE_THETA = 10000.0
EPS = 1e-6
LOG2E = 1.4426950408889634

LANES = 128
SUBLANES = 8
VMEM_LIMIT = 56 * 1024 * 1024

INT_MIN = -(2 ** 31)
INT_MAX = 2 ** 31 - 1
MAX_BISECTIONS = 32 + 2
NEG_BIG = -1e30

COL_AQ, COL_IQ, COL_DQ, COL_DK, COL_DV, COL_DZ = (i * ATTN_WIDTH for i in range(6))
COL_AK = 6 * ATTN_WIDTH
COL_AV = COL_AK + KV_WIDTH
COL_SMALL = COL_AV + KV_WIDTH
SMALL_IK, SMALL_IW, SMALL_DB, SMALL_DA = 0, 64, 96, 112
IN_TILE = 768
IN_COLS = -(-(COL_SMALL + LANES) // IN_TILE) * IN_TILE
assert ATTN_WIDTH == IDX_WIDTH == DN_WIDTH


def _cparams(sem):
    return pltpu.CompilerParams(dimension_semantics=sem, vmem_limit_bytes=VMEM_LIMIT)


def _regroup_kernel(w_ref, o_ref, *, segs):
    col = 0
    for a, b in segs:
        o_ref[:, col:col + (b - a)] = w_ref[:, a:b].astype(BF16)
        col += b - a
    o_ref[:, col:] = jnp.zeros((o_ref.shape[0], o_ref.shape[1] - col), BF16)


def _regroup_in_proj(w):
    splits = (ATTN_WIDTH, KV_WIDTH, KV_WIDTH, IDX_WIDTH, IDX_HEAD_DIM, N_IDX_HEADS,
              DN_WIDTH, DN_WIDTH, DN_WIDTH, DN_HEADS, DN_HEADS, DN_WIDTH)
    starts = np.concatenate([[0], np.cumsum(splits)])
    seg = {n: (int(starts[i]), int(starts[i + 1])) for i, n in enumerate(
        ("a_q", "a_k", "a_v", "i_q", "i_k", "i_w", "d_q", "d_k", "d_v", "d_b", "d_a", "d_z"))}
    order = ("a_q", "i_q", "d_q", "d_k", "d_v", "d_z", "a_k", "a_v", "i_k", "i_w", "d_b", "d_a")
    d, cols = w.shape
    tr = _pick(d, (256, 128, 64, 32, 16, 8))
    return pl.pallas_call(
        functools.partial(_regroup_kernel, segs=tuple(seg[n] for n in order)),
        grid=(d // tr,),
        in_specs=[pl.BlockSpec((tr, cols), lambda i: (i, 0))],
        out_specs=pl.BlockSpec((tr, IN_COLS), lambda i: (i, 0)),
        out_shape=jax.ShapeDtypeStruct((d, IN_COLS), BF16),
        compiler_params=_cparams(("parallel",)),
        name="regroup_in_proj",
    )(w)


def _pick(n, prefs):
    for p in prefs:
        if n % p == 0:
            return p
    raise ValueError(f"no tile in {prefs} divides {n}")


ROW_TILES = (512, 256, 128, 64, 32, 16, 8)


def _norm_matmul_kernel(x_ref, g_ref, w_ref, o_ref, h_ref):
    @pl.when(pl.program_id(1) == 0)
    def _():
        x = x_ref[...]
        ms = jnp.mean(x * x, axis=-1, keepdims=True)
        h_ref[...] = (x * lax.rsqrt(ms + EPS) * g_ref[...]).astype(BF16)

    o_ref[...] = jnp.dot(h_ref[...], w_ref[...], preferred_element_type=F32)


def norm_matmul(x, g, w):
    n, d = x.shape
    cols = w.shape[1]
    tm = _pick(n, ROW_TILES)
    tn = _pick(cols, (IN_TILE, 512, 256, 128))
    return pl.pallas_call(
        _norm_matmul_kernel,
        grid=(n // tm, cols // tn),
        in_specs=[pl.BlockSpec((tm, d), lambda i, j: (i, 0)),
                  pl.BlockSpec((1, d), lambda i, j: (0, 0)),
                  pl.BlockSpec((d, tn), lambda i, j: (0, j))],
        out_specs=pl.BlockSpec((tm, tn), lambda i, j: (i, j)),
        out_shape=jax.ShapeDtypeStruct((n, cols), F32),
        scratch_shapes=[pltpu.VMEM((tm, d), BF16)],
        compiler_params=_cparams(("parallel", "arbitrary")),
        name="norm_matmul",
    )(x, g.reshape(1, d), w)


def _matmul_res_kernel(*refs, n_lhs):
    a_refs, w_refs, x_ref, o_ref = refs[:n_lhs], refs[n_lhs:2 * n_lhs], refs[2 * n_lhs], refs[2 * n_lhs + 1]
    acc = x_ref[...]
    for a_ref, w_ref in zip(a_refs, w_refs):
        acc = acc + jnp.dot(a_ref[...], w_ref[...], preferred_element_type=F32)
    o_ref[...] = acc


def matmul_res(lhs, w, x):
    n, k = lhs[0].shape
    cols = w.shape[1]
    tm = _pick(n, ROW_TILES)
    tn = _pick(cols, (256,) if k > 8192 else (1024, 512, 256, 128))
    kern = functools.partial(_matmul_res_kernel, n_lhs=len(lhs))
    return pl.pallas_call(
        kern,
        grid=(n // tm, cols // tn),
        in_specs=([pl.BlockSpec((tm, k), lambda i, j: (i, 0)) for _ in lhs]
                  + [pl.BlockSpec((k, tn), lambda i, j, r=r: (r, j)) for r in range(len(lhs))]
                  + [pl.BlockSpec((tm, tn), lambda i, j: (i, j))]),
        out_specs=pl.BlockSpec((tm, tn), lambda i, j: (i, j)),
        out_shape=jax.ShapeDtypeStruct((n, cols), F32),
        compiler_params=_cparams(("parallel", "arbitrary")),
        name="matmul_res",
    )(*lhs, *([w] * len(lhs)), x)


def _rms_norm_kernel(x_ref, g_ref, o_ref):
    x = x_ref[...]
    ms = jnp.mean(x * x, axis=-1, keepdims=True)
    o_ref[...] = x * lax.rsqrt(ms + EPS) * g_ref[...]


def rms_norm_rows(x, g):
    n, d = x.shape
    tm = _pick(n, ROW_TILES)
    return pl.pallas_call(
        _rms_norm_kernel,
        grid=(n // tm,),
        in_specs=[pl.BlockSpec((tm, d), lambda i: (i, 0)), pl.BlockSpec((1, d), lambda i: (0, 0))],
        out_specs=pl.BlockSpec((tm, d), lambda i: (i, 0)),
        out_shape=jax.ShapeDtypeStruct((n, d), F32),
        compiler_params=_cparams(("parallel",)),
        name="rms_norm",
    )(x, g.reshape(1, d))


FFN_TILE = 256


def ffn_interleave(a):
    f = a.shape[-1] // 2
    lead = a.shape[:-1]
    return jnp.swapaxes(a.reshape(*lead, 2, f // FFN_TILE, FFN_TILE), -3, -2).reshape(*lead, 2 * f)


def ffn_deinterleave(a):
    f = a.shape[-1] // 2
    lead = a.shape[:-1]
    return jnp.swapaxes(a.reshape(*lead, f // FFN_TILE, 2, FFN_TILE), -3, -2).reshape(*lead, 2 * f)


def _ffn_up_kernel(x_ref, g_ref, wg_ref, wv_ref, c_ref, hist_ref, a_ref, st_ref, h_ref, e_ref, cy_ref,
                   *, tm, tn, step, halo):
    i = pl.program_id(0)
    j = pl.program_id(1)

    @pl.when(j == 0)
    def _():
        x = x_ref[...]
        ms = jnp.mean(x * x, axis=-1, keepdims=True)
        h_ref[...] = (x * lax.rsqrt(ms + EPS) * g_ref[...]).astype(BF16)

    h = h_ref[...]
    up = jnp.concatenate([jnp.dot(h, wg_ref[...], preferred_element_type=F32),
                          jnp.dot(h, wv_ref[...], preferred_element_type=F32)], axis=1)
    e_ref[halo:halo + tm, :] = up

    @pl.when(i == 0)
    def _():
        e_ref[0:halo, :] = hist_ref[...]

    @pl.when(i > 0)
    def _():
        e_ref[0:halo, :] = cy_ref[j]

    c = c_ref[...]
    y = (e_ref[halo - 2 * step:halo - 2 * step + tm, :] * c[0:1, :]
         + e_ref[halo - step:halo - step + tm, :] * c[1:2, :]
         + up * c[2:3, :])
    tail = e_ref[tm:tm + halo, :]
    cy_ref[j] = tail
    st_ref[...] = tail
    yg = y[:, :tn]
    a_ref[...] = (yg * jax.nn.sigmoid(yg) * y[:, tn:]).astype(BF16)


def ffn_up(x, g, w_up, conv_w, hist, step):
    n, d = x.shape
    f2 = w_up.shape[1]
    f = f2 // 2
    halo = hist.shape[0]
    tm = _pick(n, ROW_TILES)
    tn = FFN_TILE
    nj = f // tn
    kern = functools.partial(_ffn_up_kernel, tm=tm, tn=tn, step=step, halo=halo)
    act, tail = pl.pallas_call(
        kern,
        grid=(n // tm, nj),
        in_specs=[pl.BlockSpec((tm, d), lambda i, j: (i, 0)),
                  pl.BlockSpec((1, d), lambda i, j: (0, 0)),
                  pl.BlockSpec((d, tn), lambda i, j: (0, j)),
                  pl.BlockSpec((d, tn), lambda i, j: (0, nj + j)),
                  pl.BlockSpec((FFN_CONV, 2 * tn), lambda i, j: (0, j)),
                  pl.BlockSpec((halo, 2 * tn), lambda i, j: (0, j))],
        out_specs=[pl.BlockSpec((tm, tn), lambda i, j: (i, j)),
                   pl.BlockSpec((None, halo, 2 * tn), lambda i, j: (i, 0, j))],
        out_shape=[jax.ShapeDtypeStruct((n, f), BF16),
                   jax.ShapeDtypeStruct((n // tm, halo, f2), F32)],
        scratch_shapes=[pltpu.VMEM((tm, d), BF16),
                        pltpu.VMEM((tm + halo, 2 * tn), F32),
                        pltpu.VMEM((nj, halo, 2 * tn), F32)],
        compiler_params=_cparams(("arbitrary", "arbitrary")),
        name="ffn_up",
    )(x, g.reshape(1, d), w_up, w_up, conv_w, hist)
    return act, tail[-1]


def _attn_prep_kernel(aq_ref, iq_ref, ak_ref, av_ref, sm_ref, cosa_ref, sina_ref, cosi_ref, sini_ref,
                      q_ref, iqo_ref, kf_ref, kb_ref, vb_ref, ik2_ref, smo_ref, *, tm):
    cosa, sina = cosa_ref[...], sina_ref[...]
    cosi, sini = cosi_ref[...], sini_ref[...]
    lane = lax.broadcasted_iota(I32, (tm, LANES), 1)
    first_half = (lane & (IDX_HEAD_DIM - 1)) < IDX_HEAD_DIM // 2

    def rot_head(x):
        return x * cosa + pltpu.roll(x, HEAD_DIM // 2, 1) * sina

    def rot_idx(x):
        partner = jnp.where(first_half, pltpu.roll(x, LANES - IDX_HEAD_DIM // 2, 1), pltpu.roll(x, IDX_HEAD_DIM // 2, 1))
        return x * cosi + partner * sini

    qscale = (HEAD_DIM ** -0.5) * LOG2E
    for h in range(N_HEADS):
        sl = slice(h * LANES, (h + 1) * LANES)
        q_ref[:, sl] = (rot_head(aq_ref[:, sl]) * qscale).astype(BF16)
        iqo_ref[:, sl] = rot_idx(iq_ref[:, sl]).astype(BF16)
    for n in range(N_KV_HEADS):
        sl = slice(n * LANES, (n + 1) * LANES)
        kr = rot_head(ak_ref[:, sl])
        kf_ref[:, sl] = kr
        kb_ref[:, sl] = kr.astype(BF16)
    vb_ref[...] = av_ref[...].astype(BF16)
    sm = sm_ref[...]
    ik = jnp.where(lane < SMALL_IW, rot_idx(sm), 0.0)
    smo_ref[...] = jnp.where(lane < SMALL_IW, ik, jnp.where(lane < SMALL_DB, sm * IDX_SCALE, sm))
    ik2_ref[:, :LANES] = ik.astype(BF16)
    ik2_ref[:, LANES:] = pltpu.roll(ik, IDX_HEAD_DIM, 1).astype(BF16)


def attn_prep(proj, pos):
    n = proj.shape[0]
    tm = _pick(n, (256, 128, 64, 32, 16, 8))
    posf = pos.astype(F32)[:, None]

    def tables(half):
        inv_freq = jnp.float32(ROPE_THETA) ** (-jnp.arange(half, dtype=F32) / half)
        ang = posf * inv_freq[None, :]
        cos, sin = jnp.cos(ang), jnp.sin(ang)
        rep = LANES // (2 * half)
        return jnp.tile(jnp.concatenate([cos, cos], axis=1), (1, rep)), jnp.tile(jnp.concatenate([-sin, sin], axis=1), (1, rep))

    cosa, sina = tables(HEAD_DIM // 2)
    cosi, sini = tables(IDX_HEAD_DIM // 2)
    wide = lambda c: pl.BlockSpec((tm, ATTN_WIDTH), lambda i, c=c: (i, c // ATTN_WIDTH))
    kvw = lambda c: pl.BlockSpec((tm, KV_WIDTH), lambda i, c=c: (i, c // KV_WIDTH))
    tab = pl.BlockSpec((tm, LANES), lambda i: (i, 0))
    row = lambda w: pl.BlockSpec((tm, w), lambda i: (i, 0))
    return pl.pallas_call(
        functools.partial(_attn_prep_kernel, tm=tm),
        grid=(n // tm,),
        in_specs=[wide(COL_AQ), wide(COL_IQ), kvw(COL_AK), kvw(COL_AV),
                  pl.BlockSpec((tm, LANES), lambda i: (i, COL_SMALL // LANES)), tab, tab, tab, tab],
        out_specs=[row(ATTN_WIDTH), row(IDX_WIDTH), row(KV_WIDTH), row(KV_WIDTH), row(KV_WIDTH), row(2 * LANES), row(LANES)],
        out_shape=[jax.ShapeDtypeStruct((n, ATTN_WIDTH), BF16),
                   jax.ShapeDtypeStruct((n, IDX_WIDTH), BF16),
                   jax.ShapeDtypeStruct((n, KV_WIDTH), F32),
                   jax.ShapeDtypeStruct((n, KV_WIDTH), BF16),
                   jax.ShapeDtypeStruct((n, KV_WIDTH), BF16),
                   jax.ShapeDtypeStruct((n, 2 * LANES), BF16),
                   jax.ShapeDtypeStruct((n, LANES), F32)],
        compiler_params=_cparams(("parallel",)),
        name="attn_prep",
    )(proj, proj, proj, proj, proj, cosa, sina, cosi, sini)


IDX_PAIR_GROUP = 8
N_IDX_PAIRS = N_IDX_HEADS // 2


def _dsa_kernel(tb_ref, tqb_ref, tph_ref, tkb_ref, tnkb_ref, tikb_ref, tkvb_ref,
                iq_ref, iw_ref, ik_ref, q_ref, k_ref, v_ref, o_ref,
                keys_ref, sc_ref, kmax_ref, k2nd_ref, thr_ref, pcut_ref, wb_ref, iqs_ref, qs_ref, m_ref, acc_ref,
                *, tq, tk, n_keys, past, topk, pos_bits):
    s = pl.program_id(0)
    ph = tph_ref[s]
    kb = tkb_ref[s]
    nkb = tnkb_ref[s]
    q0 = tqb_ref[s] * tq + past
    ncol = tk // LANES
    grows = IDX_PAIR_GROUP * tq

    @pl.when(ph == 0)
    def _scores():
        @pl.when(kb == 0)
        def _():
            w = iw_ref[0]
            for h in range(N_IDX_HEADS):
                wb_ref[h] = jnp.broadcast_to(w[:, h:h + 1], (tq, LANES))
            for p in range(N_IDX_PAIRS):
                iqs_ref[p * tq:(p + 1) * tq, :] = iq_ref[0, :, p * LANES:(p + 1) * LANES]

        ik2 = ik_ref[0]
        ik_even, ik_odd = ik2[:, :LANES], ik2[:, LANES:]
        sc_ref[...] = jnp.zeros((tq, tk), F32)

        for gi in range(N_IDX_PAIRS // IDX_PAIR_GROUP):
            lhs = iqs_ref[gi * grows:(gi + 1) * grows, :]
            raws = [lax.dot_general(lhs, kk, (((1,), (1,)), ((), ())), preferred_element_type=F32)
                    for kk in (ik_even, ik_odd)]
            part = None
            for j in range(IDX_PAIR_GROUP):
                for par in range(2):
                    wv = wb_ref[(gi * IDX_PAIR_GROUP + j) * 2 + par]
                    wv = jnp.concatenate([wv] * ncol, axis=1) if ncol > 1 else wv
                    term = wv * jnp.maximum(raws[par][j * tq:(j + 1) * tq, :], 0.0)
                    part = term if part is None else part + term
            sc_ref[...] += part

        sc = sc_ref[...] + 0.0
        bits = pltpu.bitcast(sc, I32)
        key = bits ^ ((bits >> 31) & INT_MAX)
        kpos = kb * tk + lax.broadcasted_iota(I32, (tq, tk), 1)
        qpos = q0 + lax.broadcasted_iota(I32, (tq, tk), 0)
        adm = jnp.logical_and((kpos >> CHUNK_SHIFT) <= (qpos >> CHUNK_SHIFT), kpos < n_keys)
        key = jnp.where(adm, key, INT_MIN)
        keys_ref[kb] = key
        k1 = key[:, :LANES]
        k2 = jnp.full((tq, LANES), INT_MIN, I32)
        for c in range(1, ncol):
            x = key[:, c * LANES:(c + 1) * LANES]
            k2 = jnp.maximum(k2, jnp.minimum(k1, x))
            k1 = jnp.maximum(k1, x)

        @pl.when(kb == 0)
        def _():
            kmax_ref[...] = k1
            k2nd_ref[...] = k2

        @pl.when(kb > 0)
        def _():
            o1 = kmax_ref[...]
            kmax_ref[...] = jnp.maximum(o1, k1)
            k2nd_ref[...] = jnp.maximum(jnp.minimum(o1, k1), jnp.maximum(k2nd_ref[...], k2))

    @pl.when(ph == 1)
    def _attend():
        def count_rows(pred):
            def blk(b, cnt):
                hit = pred(keys_ref[b], b).astype(I32)
                for c in range(ncol):
                    cnt = cnt + hit[:, c * LANES:(c + 1) * LANES]
                return cnt

            cnt = lax.fori_loop(0, nkb, blk, jnp.zeros((tq, LANES), I32))
            return jnp.sum(cnt, axis=1, keepdims=True)

        @pl.when(kb == 0)
        def _select():
            qrow = q0 + lax.broadcasted_iota(I32, (tq, 1), 0)
            n_adm = jnp.minimum(((qrow >> CHUNK_SHIFT) + 1) << CHUNK_SHIFT, n_keys)
            keep_all = n_adm <= topk
            lo0 = jnp.where(keep_all, INT_MIN + 1,
                            jnp.maximum(jnp.min(k2nd_ref[...], axis=1, keepdims=True), INT_MIN + 1))
            hi0 = jnp.max(kmax_ref[...], axis=1, keepdims=True) + 1
            cnt0 = count_rows(lambda x, b: x >= lo0)
            done0 = jnp.logical_or(keep_all, cnt0 == topk).astype(I32)

            def bis_cond(c):
                return jnp.logical_and(c[0] < MAX_BISECTIONS, c[5] > 0)

            def bis_body(c):
                it, lo, hi, cnt_lo, done, _ = c
                mid = (lo >> 1) + (hi >> 1) + (lo & hi & 1)
                tot = count_rows(lambda x, b: x >= mid)
                move = jnp.logical_and(tot >= topk, done == 0)
                shrink = jnp.logical_and(tot < topk, done == 0)
                cnt_new = jnp.where(move, tot, cnt_lo)
                done = jnp.where(jnp.logical_or(cnt_new == topk, mid == lo), 1, done)
                return (it + 1, jnp.where(move, mid, lo), jnp.where(shrink, mid, hi), cnt_new, done,
                        jnp.sum(1 - done))

            _, thr, _, cnt_thr, _, _ = lax.while_loop(
                bis_cond, bis_body, (jnp.int32(0), lo0, hi0, cnt0, done0, jnp.sum(1 - done0)))
            thr_ref[...] = thr
            pcut_ref[...] = jnp.full((tq, 1), INT_MAX, I32)

            tied = cnt_thr > topk

            @pl.when(jnp.max(tied.astype(I32)) > 0)
            def _ties():
                room = topk - count_rows(lambda x, b: x > thr)

                def pos_body(it, cut):
                    cand = cut + lax.shift_left(jnp.int32(1), pos_bits - 1 - it)

                    def pred(x, b):
                        kpos = b * tk + lax.broadcasted_iota(I32, (tq, tk), 1)
                        return jnp.logical_and(x == thr, kpos < cand)

                    return jnp.where(count_rows(pred) < room, cand, cut)

                cut = lax.fori_loop(0, pos_bits, pos_body, jnp.zeros((tq, 1), I32))
                pcut_ref[...] = jnp.where(tied, cut, INT_MAX)

            m_ref[...] = jnp.full(m_ref.shape, NEG_BIG, F32)
            acc_ref[...] = jnp.zeros(acc_ref.shape, F32)
            qv = q_ref[0]
            for n in range(N_KV_HEADS):
                for g in range(KV_GROUP):
                    hd = n * KV_GROUP + g
                    qs_ref[n, g * tq:(g + 1) * tq, :] = qv[:, hd * HEAD_DIM:(hd + 1) * HEAD_DIM]

        key = keys_ref[kb]
        thr = thr_ref[...]
        kpos = kb * tk + lax.broadcasted_iota(I32, (tq, tk), 1)
        keep = jnp.logical_or(key > thr, jnp.logical_and(key == thr, kpos <= pcut_ref[...]))
        bias = jnp.where(keep, 0.0, NEG_BIG).astype(F32)
        bias4 = jnp.concatenate([bias] * KV_GROUP, axis=0)
        kv = k_ref[0]
        vv = v_ref[0]
        ones = jnp.ones((tk, HEAD_DIM), BF16)
        ns = range(N_KV_HEADS)
        lgs = [lax.dot_general(qs_ref[n], kv[:, n * HEAD_DIM:(n + 1) * HEAD_DIM], (((1,), (1,)), ((), ())),
                               preferred_element_type=F32) + bias4 for n in ns]
        ps, alphas = [], []
        for n in ns:
            cols = [lgs[n][:, c * LANES:(c + 1) * LANES] for c in range(ncol)]
            mx = cols[0]
            for c in range(1, ncol):
                mx = jnp.maximum(mx, cols[c])
            m_prev = m_ref[n]
            m_new = jnp.maximum(m_prev, jnp.max(mx, axis=1, keepdims=True))
            alphas.append(jnp.exp2(m_prev - m_new))
            ps.append(jnp.concatenate([jnp.exp2(cols[c] - m_new).astype(BF16) for c in range(ncol)], axis=1))
            m_ref[n] = m_new
        for n in ns:
            ve = jnp.concatenate([vv[:, n * HEAD_DIM:(n + 1) * HEAD_DIM], ones], axis=1)
            pv = jnp.dot(ps[n], ve, preferred_element_type=F32)
            acc_ref[n] = jnp.concatenate([alphas[n], alphas[n]], axis=1) * acc_ref[n] + pv

        @pl.when(kb == nkb - 1)
        def _finish():
            for n in range(N_KV_HEADS):
                a = acc_ref[n]
                o = a[:, :HEAD_DIM] / a[:, HEAD_DIM:]
                for g in range(KV_GROUP):
                    hd = n * KV_GROUP + g
                    o_ref[0, :, hd * HEAD_DIM:(hd + 1) * HEAD_DIM] = o[g * tq:(g + 1) * tq, :].astype(BF16)


def _dsa_tables(batch, t, tq, tk, past, n_keys):
    rows = []
    for b in range(batch):
        for qb in range(t // tq):
            last_q = past + qb * tq + tq - 1
            limit = min(n_keys, (last_q // CHUNK + 1) * CHUNK)
            nkb = -(-limit // tk)
            for ph in range(2):
                for kb in range(nkb):
                    ikb = kb if ph == 0 else nkb - 1
                    kvb = 0 if ph == 0 else kb
                    rows.append((b, qb, ph, kb, nkb, ikb, kvb))
    tab = np.asarray(rows, dtype=np.int32).T
    return [jnp.asarray(tab[r]) for r in range(tab.shape[0])]


def dsa_attention(iq, iw, ik2, q, k, v, *, n_keys, past, tq, tk):
    batch, t, _ = iq.shape
    lp = ik2.shape[1]
    assert t % tq == 0 and lp % tk == 0 and tk % LANES == 0
    topk = min(IDX_TOPK_MAX, n_keys // 4)
    assert topk <= 2 * LANES
    tables = _dsa_tables(batch, t, tq, tk, past, n_keys)
    nsteps = int(tables[0].shape[0])
    nkb_max = lp // tk
    kern = functools.partial(_dsa_kernel, tq=tq, tk=tk, n_keys=n_keys, past=past, topk=topk,
                             pos_bits=max(1, (lp - 1).bit_length()))
    qmap = lambda s, tb, tqb, tph, tkb, tnkb, tikb, tkvb: (tb[s], tqb[s], 0)
    grid_spec = pltpu.PrefetchScalarGridSpec(
        num_scalar_prefetch=7,
        grid=(nsteps,),
        in_specs=[
            pl.BlockSpec((1, tq, IDX_WIDTH), qmap),
            pl.BlockSpec((1, tq, N_IDX_HEADS), qmap),
            pl.BlockSpec((1, tk, 2 * LANES), lambda s, tb, tqb, tph, tkb, tnkb, tikb, tkvb: (tb[s], tikb[s], 0)),
            pl.BlockSpec((1, tq, ATTN_WIDTH), qmap),
            pl.BlockSpec((1, tk, KV_WIDTH), lambda s, tb, tqb, tph, tkb, tnkb, tikb, tkvb: (tb[s], tkvb[s], 0)),
            pl.BlockSpec((1, tk, KV_WIDTH), lambda s, tb, tqb, tph, tkb, tnkb, tikb, tkvb: (tb[s], tkvb[s], 0)),
        ],
        out_specs=pl.BlockSpec((1, tq, ATTN_WIDTH), qmap),
        scratch_shapes=[
            pltpu.VMEM((nkb_max, tq, tk), I32),
            pltpu.VMEM((tq, tk), F32),
            pltpu.VMEM((tq, LANES), I32),
            pltpu.VMEM((tq, LANES), I32),
            pltpu.VMEM((tq, 1), I32),
            pltpu.VMEM((tq, 1), I32),
            pltpu.VMEM((N_IDX_HEADS, tq, LANES), F32),
            pltpu.VMEM((N_IDX_PAIRS * tq, LANES), BF16),
            pltpu.VMEM((N_KV_HEADS, KV_GROUP * tq, HEAD_DIM), BF16),
            pltpu.VMEM((N_KV_HEADS, KV_GROUP * tq, LANES), F32),
            pltpu.VMEM((N_KV_HEADS, KV_GROUP * tq, 2 * HEAD_DIM), F32),
        ],
    )
    return pl.pallas_call(
        kern,
        grid_spec=grid_spec,
        out_shape=jax.ShapeDtypeStruct((batch, t, ATTN_WIDTH), BF16),
        compiler_params=_cparams(("arbitrary",)),
        name="dsa_attention",
    )(*tables, iq, iw, ik2, q, k, v)


def _mm(a, b):
    return jnp.dot(a.astype(BF16), b.astype(BF16), preferred_element_type=F32)


def _mm_nt(a, b):
    return lax.dot_general(a.astype(BF16), b.astype(BF16), (((1,), (1,)), ((), ())), preferred_element_type=F32)


def _mm_tn(a, b):
    return lax.dot_general(a.astype(BF16), b.astype(BF16), (((0,), (0,)), ((), ())), preferred_element_type=F32)


def _delta_kernel(dq_ref, dk_ref, dv_ref, z_ref, bq_ref, bk_ref, bv_ref, cq_ref, ck_ref, cv_ref,
                  gc_ref, gr_ref, b_ref, gn_ref, s0_ref, o_ref, s_ref, eq_ref, ek_ref, ev_ref,
                  *, chunk, n_double):
    first = pl.program_id(1) == 0

    @pl.when(first)
    def _():
        s_ref[...] = s0_ref[...]
        for e_ref, buf_ref in ((eq_ref, bq_ref), (ek_ref, bk_ref), (ev_ref, bv_ref)):
            e_ref[0:SUBLANES, :] = buf_ref[0]

    def conv_silu(x_ref, cw_ref, e_ref):
        e_ref[SUBLANES:SUBLANES + chunk, :] = x_ref[0]
        w = cw_ref[...]
        y = e_ref[SUBLANES:SUBLANES + chunk, :] * w[DN_CONV - 1:DN_CONV, :]
        for i in range(DN_CONV - 1):
            off = SUBLANES - (DN_CONV - 1) + i
            y = y + e_ref[off:off + chunk, :] * w[i:i + 1, :]
        e_ref[0:SUBLANES, :] = e_ref[chunk:chunk + SUBLANES, :]
        return y * jax.nn.sigmoid(y)

    yq = conv_silu(dq_ref, cq_ref, eq_ref)
    yk = conv_silu(dk_ref, ck_ref, ek_ref)
    yv = conv_silu(dv_ref, cv_ref, ev_ref)

    row = lax.broadcasted_iota(I32, (chunk, chunk), 0)
    col = lax.broadcasted_iota(I32, (chunk, chunk), 1)
    eye = (row == col).astype(F32)
    gcs = gc_ref[0]
    grs = gr_ref[0, 0]
    betas = b_ref[0]
    gn = gn_ref[...]
    hs = range(DN_HEADS)
    sl = [slice(h * DN_HEAD_DIM, (h + 1) * DN_HEAD_DIM) for h in hs]

    def l2n(x):
        return x * lax.rsqrt(jnp.sum(x * x, axis=-1, keepdims=True) + EPS)

    kh = [l2n(yk[:, sl[h]]) for h in hs]
    qh = [l2n(yq[:, sl[h]]) * (DN_HEAD_DIM ** -0.5) for h in hs]
    gc = [gcs[:, h:h + 1] for h in hs]
    beta = [betas[:, h:h + 1] for h in hs]
    decay = [jnp.exp(jnp.where(row >= col, gc[h] - grs[h:h + 1, :], -jnp.inf)) for h in hs]
    kk = [_mm_nt(kh[h], kh[h]) for h in hs]
    qk = [_mm_nt(qh[h], kh[h]) * decay[h] for h in hs]
    x = [-jnp.where(row > col, beta[h] * kk[h] * decay[h], 0.0) for h in hs]
    parts = [x]
    for _ in range(n_double):
        x = [_mm(x[h], x[h]) for h in hs]
        parts.append(x)
    while len(parts) > 1:
        nxt = [[parts[i][h] + parts[i + 1][h] + _mm(parts[i][h], parts[i + 1][h]) for h in hs]
               for i in range(0, len(parts) - 1, 2)]
        if len(parts) % 2:
            nxt.append(parts[-1])
        parts = nxt
    t = [eye + parts[0][h] for h in hs]
    eg = [jnp.exp(gc[h]) for h in hs]
    uw = [_mm(t[h], jnp.concatenate([yv[:, sl[h]] * beta[h], kh[h] * (beta[h] * eg[h])], axis=1)) for h in hs]
    s = [s_ref[0, h] for h in hs]
    ws_qs = [_mm(jnp.concatenate([uw[h][:, DN_HEAD_DIM:], qh[h] * eg[h]], axis=0), s[h]) for h in hs]
    v_new = [uw[h][:, :DN_HEAD_DIM] - ws_qs[h][:chunk] for h in hs]
    gl = [gc[h][chunk - 1:chunk, :] for h in hs]
    o = [ws_qs[h][chunk:] + _mm(qk[h], v_new[h]) for h in hs]
    upd = [_mm_tn(kh[h] * jnp.exp(gl[h] - gc[h]), v_new[h]) for h in hs]
    for h in hs:
        s_ref[0, h] = s[h] * jnp.exp(gl[h]) + upd[h]
        ms = jnp.mean(o[h] * o[h], axis=-1, keepdims=True)
        on = o[h] * lax.rsqrt(ms + EPS) * gn
        z = z_ref[0, :, sl[h]]
        o_ref[0, :, sl[h]] = (on * (z * jax.nn.sigmoid(z))).astype(BF16)


def delta_rule(proj, dn_buf, conv_w, gc, beta, gnorm, state, chunk):
    batch, t, _ = proj.shape
    n = t // chunk
    n_double = max(0, (chunk - 1).bit_length() - 1)
    gr = jnp.swapaxes(gc.reshape(batch, n, chunk, DN_HEADS), 2, 3)
    buf8 = jnp.pad(dn_buf, ((0, 0), (SUBLANES - (DN_CONV - 1), 0), (0, 0)))
    kern = functools.partial(_delta_kernel, chunk=chunk, n_double=n_double)
    pcol = lambda c: pl.BlockSpec((1, chunk, DN_WIDTH), lambda b, i, c=c: (b, i, c // DN_WIDTH))
    bcol = lambda j: pl.BlockSpec((1, SUBLANES, DN_WIDTH), lambda b, i, j=j: (b, 0, j))
    wcol = lambda j: pl.BlockSpec((DN_CONV, DN_WIDTH), lambda b, i, j=j: (0, j))
    wide = pl.BlockSpec((1, chunk, DN_WIDTH), lambda b, i: (b, i, 0))
    narrow = pl.BlockSpec((1, chunk, DN_HEADS), lambda b, i: (b, i, 0))
    st = pl.BlockSpec((1, DN_HEADS, DN_HEAD_DIM, DN_HEAD_DIM), lambda b, i: (b, 0, 0, 0))
    return pl.pallas_call(
        kern,
        grid=(batch, n),
        in_specs=[pcol(COL_DQ), pcol(COL_DK), pcol(COL_DV), pcol(COL_DZ),
                  bcol(0), bcol(1), bcol(2), wcol(0), wcol(1), wcol(2),
                  narrow,
                  pl.BlockSpec((1, 1, DN_HEADS, chunk), lambda b, i: (b, i, 0, 0)),
                  narrow,
                  pl.BlockSpec((1, DN_HEAD_DIM), lambda b, i: (0, 0)),
                  st],
        out_specs=[wide, st],
        out_shape=[jax.ShapeDtypeStruct((batch, t, DN_WIDTH), BF16),
                   jax.ShapeDtypeStruct(state.shape, F32)],
        scratch_shapes=[pltpu.VMEM((chunk + SUBLANES, DN_WIDTH), F32)] * 3,
        compiler_params=_cparams(("parallel", "arbitrary")),
        name="delta_rule",
    )(proj, proj, proj, proj, buf8, buf8, buf8, conv_w, conv_w, conv_w,
      gc, gr, beta, gnorm.reshape(1, DN_HEAD_DIM), state)


def _layer(x, past_k, past_v, past_ik, dn_buf, dn_state, ffn_buf, lw, *, attn_tiles):
    (norm1_g, w_in, dn_conv_w, dn_a_log, dn_dt_bias, dn_norm_g, w_out, norm2_g, ffn_w_up, ffn_conv_w,
     ffn_w_down) = lw
    b, t, d = x.shape
    past = past_k.shape[1]
    n_keys = past + t
    x2 = x.reshape(b * t, d)

    proj = norm_matmul(x2, norm1_g, w_in)
    pos = jnp.tile(past + jnp.arange(t, dtype=I32), b)
    q_bf, iq_bf, k_f32, k_bf, v_bf, ik2_bf, small = attn_prep(proj, pos)
    proj3 = proj.reshape(b, t, IN_COLS)
    small = small.reshape(b, t, LANES)
    new_k = k_f32.reshape(b, t, N_KV_HEADS, HEAD_DIM)
    new_v = proj3[:, :, COL_AV:COL_AV + KV_WIDTH].reshape(b, t, N_KV_HEADS, HEAD_DIM)
    new_ik = small[:, :, SMALL_IK:SMALL_IK + IDX_HEAD_DIM]
    i_w = small[:, :, SMALL_IW:SMALL_IW + N_IDX_HEADS]

    tq, tk = attn_tiles
    lp = -(-n_keys // tk) * tk

    def with_past(old_bf, new_bf):
        allk = jnp.concatenate([old_bf, new_bf.reshape(b, t, -1)], axis=1)
        return jnp.pad(allk, ((0, 0), (0, lp - n_keys), (0, 0)))

    pik = past_ik.astype(BF16)
    zpad = jnp.zeros_like(pik)
    past_ik2 = jnp.concatenate([pik, zpad, zpad, pik], axis=-1)
    attn = dsa_attention(
        iq_bf.reshape(b, t, IDX_WIDTH), i_w, with_past(past_ik2, ik2_bf), q_bf.reshape(b, t, ATTN_WIDTH),
        with_past(past_k.reshape(b, past, KV_WIDTH).astype(BF16), k_bf),
        with_past(past_v.reshape(b, past, KV_WIDTH).astype(BF16), v_bf),
        n_keys=n_keys, past=past, tq=tq, tk=tk)

    new_dn_buf = jnp.concatenate([dn_buf, proj3[:, :, COL_DQ:COL_DQ + 3 * DN_WIDTH]], axis=1)[:, t:] if t < DN_CONV - 1 \
        else proj3[:, t - (DN_CONV - 1):, COL_DQ:COL_DQ + 3 * DN_WIDTH]
    beta = jax.nn.sigmoid(small[:, :, SMALL_DB:SMALL_DB + DN_HEADS])
    g = -jnp.exp(dn_a_log) * jax.nn.softplus(small[:, :, SMALL_DA:SMALL_DA + DN_HEADS] + dn_dt_bias)
    chunk = CHUNK if t % CHUNK == 0 else t
    gc = jnp.cumsum(g.reshape(b, t // chunk, chunk, DN_HEADS), axis=2).reshape(b, t, DN_HEADS)
    o_dn, new_state = delta_rule(proj3, dn_buf, dn_conv_w, gc, beta, dn_norm_g, dn_state, chunk)

    x2 = matmul_res([attn.reshape(b * t, ATTN_WIDTH), o_dn.reshape(b * t, DN_WIDTH)], w_out, x2)

    f2 = ffn_w_up.shape[1]
    xt = jnp.swapaxes(x2.reshape(b, t, d), 0, 1).reshape(t * b, d)
    halo = max(SUBLANES, (FFN_CONV - 1) * b)
    hist = jnp.swapaxes(ffn_buf, 0, 1).reshape((FFN_CONV - 1) * b, f2)
    hist = ffn_interleave(jnp.pad(hist, ((halo - hist.shape[0], 0), (0, 0))))
    act, tail = ffn_up(xt, norm2_g, ffn_w_up, ffn_conv_w, hist, step=b)
    tail = ffn_deinterleave(tail[halo - (FFN_CONV - 1) * b:])
    new_ffn_buf = jnp.swapaxes(tail.reshape(FFN_CONV - 1, b, f2), 0, 1)
    act = jnp.swapaxes(act.reshape(t, b, -1), 0, 1).reshape(b * t, -1)
    x2 = matmul_res([act], ffn_w_down, x2)
    return x2.reshape(b, t, d), new_k, new_v, new_ik, new_dn_buf, new_state, new_ffn_buf


def kernel(x_prompt, x_sample, cache_k, cache_v, cache_kidx, state_dn_conv, state_dn, state_ffn_conv,
           norm1_g, w_in, dn_conv_w, dn_a_log, dn_dt_bias, dn_norm_g, w_out, norm2_g,
           ffn_w_up, ffn_conv_w, ffn_w_down, final_g):
    depth = w_in.shape[0]
    bp, tp, d = x_prompt.shape
    bs, ts, _ = x_sample.shape
    f2 = ffn_w_up.shape[2]
    zk = jnp.zeros((bp, 0, N_KV_HEADS, HEAD_DIM), F32)
    zik = jnp.zeros((bp, 0, IDX_HEAD_DIM), F32)
    z_dn_buf = jnp.zeros((bp, DN_CONV - 1, 3 * DN_WIDTH), F32)
    z_dn_state = jnp.zeros((bp, DN_HEADS, DN_HEAD_DIM, DN_HEAD_DIM), F32)
    z_ffn_buf = jnp.zeros((bp, FFN_CONV - 1, f2), F32)

    tq_p = _pick(tp, (256, 128, 64, 32, 16))
    tk_p = _pick(tp, (512, 256, 128))
    n_keys_s = cache_k.shape[2] + ts
    tk_s = -(-n_keys_s // LANES) * LANES

    xp, xs = x_prompt, x_sample
    st_p, st_s = [], []
    for l in range(depth):
        lw = (norm1_g[l], _regroup_in_proj(w_in[l]), dn_conv_w[l], dn_a_log[l], dn_dt_bias[l],
              dn_norm_g[l], w_out[l].astype(BF16), norm2_g[l], ffn_w_up[l].astype(BF16),
              ffn_interleave(ffn_conv_w[l]), ffn_w_down[l].astype(BF16))
        xp, *sp = _layer(xp, zk, zk, zik, z_dn_buf, z_dn_state, z_ffn_buf, lw, attn_tiles=(tq_p, tk_p))
        xs, *ss = _layer(xs, cache_k[l], cache_v[l], cache_kidx[l], state_dn_conv[l], state_dn[l],
                         state_ffn_conv[l], lw, attn_tiles=(ts, tk_s))
        st_p.append(sp)
        st_s.append(ss)

    y_prompt = rms_norm_rows(xp.reshape(bp * tp, d), final_g).reshape(bp, tp, d)
    y_sample = rms_norm_rows(xs.reshape(bs * ts, d), final_g).reshape(bs, ts, d)

    def stacked(states, i):
        return jnp.stack([s[i] for s in states], axis=0)

    return (y_prompt, y_sample,
            *[stacked(st_p, i) for i in range(6)],
            *[stacked(st_s, i) for i in range(6)])
```

```python
import functools

import numpy as np
import jax
import jax.numpy as jnp
from jax import lax
from jax.experimental import pallas as pl
from jax.experimental.pallas import tpu as pltpu

F32 = jnp.float32
BF16 = jnp.bfloat16
I32 = jnp.int32

CHUNK = 64
CHUNK_SHIFT = CHUNK.bit_length() - 1
HEAD_DIM = 128
N_HEADS = 16
N_KV_HEADS = 4
KV_GROUP = N_HEADS // N_KV_HEADS
ATTN_WIDTH = N_HEADS * HEAD_DIM
KV_WIDTH = N_KV_HEADS * HEAD_DIM
N_IDX_HEADS = 32
IDX_HEAD_DIM = 64
IDX_WIDTH = N_IDX_HEADS * IDX_HEAD_DIM
IDX_SCALE = IDX_WIDTH ** -0.5
IDX_TOPK_MAX = 256
DN_HEADS = 16
DN_HEAD_DIM = 128
DN_WIDTH = DN_HEADS * DN_HEAD_DIM
DN_CONV = 4
FFN_CONV = 3
ROPE_THETA = 10000.0
EPS = 1e-6
LOG2E = 1.4426950408889634

LANES = 128
SUBLANES = 8
VMEM_LIMIT = 56 * 1024 * 1024

INT_MIN = -(2 ** 31)
INT_MAX = 2 ** 31 - 1
MAX_BISECTIONS = 32 + 2
NEG_BIG = -1e30

COL_AQ, COL_IQ, COL_DQ, COL_DK, COL_DV, COL_DZ = (i * ATTN_WIDTH for i in range(6))
COL_AK = 6 * ATTN_WIDTH
COL_AV = COL_AK + KV_WIDTH
COL_SMALL = COL_AV + KV_WIDTH
SMALL_IK, SMALL_IW, SMALL_DB, SMALL_DA = 0, 64, 96, 112
IN_TILE = 768
IN_COLS = -(-(COL_SMALL + LANES) // IN_TILE) * IN_TILE
assert ATTN_WIDTH == IDX_WIDTH == DN_WIDTH


def _cparams(sem):
    return pltpu.CompilerParams(dimension_semantics=sem, vmem_limit_bytes=VMEM_LIMIT)


def _regroup_kernel(w_ref, o_ref, *, segs):
    col = 0
    for a, b in segs:
        o_ref[:, col:col + (b - a)] = w_ref[:, a:b].astype(BF16)
        col += b - a
    o_ref[:, col:] = jnp.zeros((o_ref.shape[0], o_ref.shape[1] - col), BF16)


def _regroup_in_proj(w, layer):
    splits = (ATTN_WIDTH, KV_WIDTH, KV_WIDTH, IDX_WIDTH, IDX_HEAD_DIM, N_IDX_HEADS,
              DN_WIDTH, DN_WIDTH, DN_WIDTH, DN_HEADS, DN_HEADS, DN_WIDTH)
    starts = np.concatenate([[0], np.cumsum(splits)])
    seg = {n: (int(starts[i]), int(starts[i + 1])) for i, n in enumerate(
        ("a_q", "a_k", "a_v", "i_q", "i_k", "i_w", "d_q", "d_k", "d_v", "d_b", "d_a", "d_z"))}
    order = ("a_q", "i_q", "d_q", "d_k", "d_v", "d_z", "a_k", "a_v", "i_k", "i_w", "d_b", "d_a")
    _, d, cols = w.shape
    tr = _pick(d, (256, 128, 64, 32, 16, 8))
    return pl.pallas_call(
        functools.partial(_regroup_kernel, segs=tuple(seg[n] for n in order)),
        grid=(d // tr,),
        in_specs=[pl.BlockSpec((None, tr, cols), lambda i: (layer, i, 0))],
        out_specs=pl.BlockSpec((tr, IN_COLS), lambda i: (i, 0)),
        out_shape=jax.ShapeDtypeStruct((d, IN_COLS), BF16),
        compiler_params=_cparams(("parallel",)),
        name="regroup_in_proj",
    )(w)


def _pick(n, prefs):
    for p in prefs:
        if n % p == 0:
            return p
    raise ValueError(f"no tile in {prefs} divides {n}")


ROW_TILES = (512, 256, 128, 64, 32, 16, 8)


def _norm_matmul_kernel(x_ref, g_ref, w_ref, o_ref, h_ref):
    @pl.when(pl.program_id(1) == 0)
    def _():
        x = x_ref[...]
        ms = jnp.mean(x * x, axis=-1, keepdims=True)
        h_ref[...] = (x * lax.rsqrt(ms + EPS) * g_ref[...]).astype(BF16)

    o_ref[...] = jnp.dot(h_ref[...], w_ref[...], preferred_element_type=F32)


def norm_matmul(x, g, w):
    n, d = x.shape
    cols = w.shape[1]
    tm = _pick(n, ROW_TILES)
    tn = _pick(cols, (IN_TILE, 512, 256, 128))
    return pl.pallas_call(
        _norm_matmul_kernel,
        grid=(n // tm, cols // tn),
        in_specs=[pl.BlockSpec((tm, d), lambda i, j: (i, 0)),
                  pl.BlockSpec((1, d), lambda i, j: (0, 0)),
                  pl.BlockSpec((d, tn), lambda i, j: (0, j))],
        out_specs=pl.BlockSpec((tm, tn), lambda i, j: (i, j)),
        out_shape=jax.ShapeDtypeStruct((n, cols), F32),
        scratch_shapes=[pltpu.VMEM((tm, d), BF16)],
        compiler_params=_cparams(("parallel", "arbitrary")),
        name="norm_matmul",
    )(x, g.reshape(1, d), w)


def _matmul_res_kernel(*refs, n_lhs):
    a_refs, w_refs, x_ref, o_ref = refs[:n_lhs], refs[n_lhs:2 * n_lhs], refs[2 * n_lhs], refs[2 * n_lhs + 1]
    acc = x_ref[...]
    for a_ref, w_ref in zip(a_refs, w_refs):
        acc = acc + jnp.dot(a_ref[...], w_ref[...], preferred_element_type=F32)
    o_ref[...] = acc


def matmul_res(lhs, w, x, layer):
    n, k = lhs[0].shape
    cols = w.shape[2]
    tm = _pick(n, ROW_TILES)
    tn = _pick(cols, (256,) if k > 8192 else (1024, 512, 256, 128))
    kern = functools.partial(_matmul_res_kernel, n_lhs=len(lhs))
    return pl.pallas_call(
        kern,
        grid=(n // tm, cols // tn),
        in_specs=([pl.BlockSpec((tm, k), lambda i, j: (i, 0)) for _ in lhs]
                  + [pl.BlockSpec((None, k, tn), lambda i, j, r=r: (layer, r, j)) for r in range(len(lhs))]
                  + [pl.BlockSpec((tm, tn), lambda i, j: (i, j))]),
        out_specs=pl.BlockSpec((tm, tn), lambda i, j: (i, j)),
        out_shape=jax.ShapeDtypeStruct((n, cols), F32),
        compiler_params=_cparams(("parallel", "arbitrary")),
        name="matmul_res",
    )(*lhs, *([w] * len(lhs)), x)


def _rms_norm_kernel(x_ref, g_ref, o_ref):
    x = x_ref[...]
    ms = jnp.mean(x * x, axis=-1, keepdims=True)
    o_ref[...] = x * lax.rsqrt(ms + EPS) * g_ref[...]


def rms_norm_rows(x, g):
    n, d = x.shape
    tm = _pick(n, ROW_TILES)
    return pl.pallas_call(
        _rms_norm_kernel,
        grid=(n // tm,),
        in_specs=[pl.BlockSpec((tm, d), lambda i: (i, 0)), pl.BlockSpec((1, d), lambda i: (0, 0))],
        out_specs=pl.BlockSpec((tm, d), lambda i: (i, 0)),
        out_shape=jax.ShapeDtypeStruct((n, d), F32),
        compiler_params=_cparams(("parallel",)),
        name="rms_norm",
    )(x, g.reshape(1, d))


FFN_TILE = 256


def ffn_interleave(a):
    f = a.shape[-1] // 2
    lead = a.shape[:-1]
    return jnp.swapaxes(a.reshape(*lead, 2, f // FFN_TILE, FFN_TILE), -3, -2).reshape(*lead, 2 * f)


def ffn_deinterleave(a):
    f = a.shape[-1] // 2
    lead = a.shape[:-1]
    return jnp.swapaxes(a.reshape(*lead, f // FFN_TILE, 2, FFN_TILE), -3, -2).reshape(*lead, 2 * f)


def _ffn_up_kernel(x_ref, g_ref, wg_ref, wv_ref, c_ref, hist_ref, a_ref, st_ref, h_ref, e_ref, cy_ref,
                   *, tm, tn, step, halo):
    i = pl.program_id(0)
    j = pl.program_id(1)

    @pl.when(j == 0)
    def _():
        x = x_ref[...]
        ms = jnp.mean(x * x, axis=-1, keepdims=True)
        h_ref[...] = (x * lax.rsqrt(ms + EPS) * g_ref[...]).astype(BF16)

    h = h_ref[...]
    up = jnp.concatenate([jnp.dot(h, wg_ref[...], preferred_element_type=F32),
                          jnp.dot(h, wv_ref[...], preferred_element_type=F32)], axis=1)
    e_ref[halo:halo + tm, :] = up

    @pl.when(i == 0)
    def _():
        e_ref[0:halo, :] = hist_ref[...]

    @pl.when(i > 0)
    def _():
        e_ref[0:halo, :] = cy_ref[j]

    c = c_ref[...]
    y = (e_ref[halo - 2 * step:halo - 2 * step + tm, :] * c[0:1, :]
         + e_ref[halo - step:halo - step + tm, :] * c[1:2, :]
         + up * c[2:3, :])
    tail = e_ref[tm:tm + halo, :]
    cy_ref[j] = tail
    st_ref[...] = tail
    yg = y[:, :tn]
    a_ref[...] = (yg * jax.nn.sigmoid(yg) * y[:, tn:]).astype(BF16)


def ffn_up(x, g, w_up, conv_w, hist, step, layer):
    n, d = x.shape
    f2 = w_up.shape[2]
    f = f2 // 2
    halo = hist.shape[0]
    tm = _pick(n, ROW_TILES)
    tn = FFN_TILE
    nj = f // tn
    kern = functools.partial(_ffn_up_kernel, tm=tm, tn=tn, step=step, halo=halo)
    act, tail = pl.pallas_call(
        kern,
        grid=(n // tm, nj),
        in_specs=[pl.BlockSpec((tm, d), lambda i, j: (i, 0)),
                  pl.BlockSpec((1, d), lambda i, j: (0, 0)),
                  pl.BlockSpec((None, d, tn), lambda i, j: (layer, 0, j)),
                  pl.BlockSpec((None, d, tn), lambda i, j: (layer, 0, nj + j)),
                  pl.BlockSpec((FFN_CONV, 2 * tn), lambda i, j: (0, j)),
                  pl.BlockSpec((halo, 2 * tn), lambda i, j: (0, j))],
        out_specs=[pl.BlockSpec((tm, tn), lambda i, j: (i, j)),
                   pl.BlockSpec((None, halo, 2 * tn), lambda i, j: (i, 0, j))],
        out_shape=[jax.ShapeDtypeStruct((n, f), BF16),
                   jax.ShapeDtypeStruct((n // tm, halo, f2), F32)],
        scratch_shapes=[pltpu.VMEM((tm, d), BF16),
                        pltpu.VMEM((tm + halo, 2 * tn), F32),
                        pltpu.VMEM((nj, halo, 2 * tn), F32)],
        compiler_params=_cparams(("arbitrary", "arbitrary")),
        name="ffn_up",
    )(x, g.reshape(1, d), w_up, w_up, conv_w, hist)
    return act, tail[-1]


def _attn_prep_kernel(aq_ref, iq_ref, ak_ref, av_ref, sm_ref, cosa_ref, sina_ref, cosi_ref, sini_ref,
                      q_ref, iqo_ref, kf_ref, kb_ref, vb_ref, ik2_ref, smo_ref, *, tm):
    cosa, sina = cosa_ref[...], sina_ref[...]
    cosi, sini = cosi_ref[...], sini_ref[...]
    lane = lax.broadcasted_iota(I32, (tm, LANES), 1)
    first_half = (lane & (IDX_HEAD_DIM - 1)) < IDX_HEAD_DIM // 2

    def rot_head(x):
        return x * cosa + pltpu.roll(x, HEAD_DIM // 2, 1) * sina

    def rot_idx(x):
        partner = jnp.where(first_half, pltpu.roll(x, LANES - IDX_HEAD_DIM // 2, 1), pltpu.roll(x, IDX_HEAD_DIM // 2, 1))
        return x * cosi + partner * sini

    qscale = (HEAD_DIM ** -0.5) * LOG2E
    for h in range(N_HEADS):
        sl = slice(h * LANES, (h + 1) * LANES)
        q_ref[:, sl] = (rot_head(aq_ref[:, sl]) * qscale).astype(BF16)
        iqo_ref[:, sl] = rot_idx(iq_ref[:, sl]).astype(BF16)
    for n in range(N_KV_HEADS):
        sl = slice(n * LANES, (n + 1) * LANES)
        kr = rot_head(ak_ref[:, sl])
        kf_ref[:, sl] = kr
        kb_ref[:, sl] = kr.astype(BF16)
    vb_ref[...] = av_ref[...].astype(BF16)
    sm = sm_ref[...]
    ik = jnp.where(lane < SMALL_IW, rot_idx(sm), 0.0)
    smo_ref[...] = jnp.where(lane < SMALL_IW, ik, jnp.where(lane < SMALL_DB, sm * IDX_SCALE, sm))
    ik2_ref[:, :LANES] = ik.astype(BF16)
    ik2_ref[:, LANES:] = pltpu.roll(ik, IDX_HEAD_DIM, 1).astype(BF16)


def attn_prep(proj, pos):
    n = proj.shape[0]
    tm = _pick(n, (256, 128, 64, 32, 16, 8))
    posf = pos.astype(F32)[:, None]

    def tables(half):
        inv_freq = jnp.float32(ROPE_THETA) ** (-jnp.arange(half, dtype=F32) / half)
        ang = posf * inv_freq[None, :]
        cos, sin = jnp.cos(ang), jnp.sin(ang)
        rep = LANES // (2 * half)
        return jnp.tile(jnp.concatenate([cos, cos], axis=1), (1, rep)), jnp.tile(jnp.concatenate([-sin, sin], axis=1), (1, rep))

    cosa, sina = tables(HEAD_DIM // 2)
    cosi, sini = tables(IDX_HEAD_DIM // 2)
    wide = lambda c: pl.BlockSpec((tm, ATTN_WIDTH), lambda i, c=c: (i, c // ATTN_WIDTH))
    kvw = lambda c: pl.BlockSpec((tm, KV_WIDTH), lambda i, c=c: (i, c // KV_WIDTH))
    tab = pl.BlockSpec((tm, LANES), lambda i: (i, 0))
    row = lambda w: pl.BlockSpec((tm, w), lambda i: (i, 0))
    return pl.pallas_call(
        functools.partial(_attn_prep_kernel, tm=tm),
        grid=(n // tm,),
        in_specs=[wide(COL_AQ), wide(COL_IQ), kvw(COL_AK), kvw(COL_AV),
                  pl.BlockSpec((tm, LANES), lambda i: (i, COL_SMALL // LANES)), tab, tab, tab, tab],
        out_specs=[row(ATTN_WIDTH), row(IDX_WIDTH), row(KV_WIDTH), row(KV_WIDTH), row(KV_WIDTH), row(2 * LANES), row(LANES)],
        out_shape=[jax.ShapeDtypeStruct((n, ATTN_WIDTH), BF16),
                   jax.ShapeDtypeStruct((n, IDX_WIDTH), BF16),
                   jax.ShapeDtypeStruct((n, KV_WIDTH), F32),
                   jax.ShapeDtypeStruct((n, KV_WIDTH), BF16),
                   jax.ShapeDtypeStruct((n, KV_WIDTH), BF16),
                   jax.ShapeDtypeStruct((n, 2 * LANES), BF16),
                   jax.ShapeDtypeStruct((n, LANES), F32)],
        compiler_params=_cparams(("parallel",)),
        name="attn_prep",
    )(proj, proj, proj, proj, proj, cosa, sina, cosi, sini)


IDX_PAIR_GROUP = 8
N_IDX_PAIRS = N_IDX_HEADS // 2


def _dsa_kernel(tb_ref, tqb_ref, tph_ref, tkb_ref, tnkb_ref, tikb_ref, tkvb_ref,
                iq_ref, iw_ref, ik_ref, q_ref, k_ref, v_ref, o_ref,
                keys_ref, sc_ref, kmax_ref, k2nd_ref, thr_ref, pcut_ref, wb_ref, iqs_ref, qs_ref, m_ref, acc_ref,
                *, tq, tk, n_keys, past, topk, pos_bits):
    s = pl.program_id(0)
    ph = tph_ref[s]
    kb = tkb_ref[s]
    nkb = tnkb_ref[s]
    q0 = tqb_ref[s] * tq + past
    ncol = tk // LANES
    grows = IDX_PAIR_GROUP * tq

    @pl.when(ph == 0)
    def _scores():
        @pl.when(kb == 0)
        def _():
            w = iw_ref[0]
            for h in range(N_IDX_HEADS):
                wb_ref[h] = jnp.broadcast_to(w[:, h:h + 1], (tq, LANES))
            for p in range(N_IDX_PAIRS):
                iqs_ref[p * tq:(p + 1) * tq, :] = iq_ref[0, :, p * LANES:(p + 1) * LANES]

        ik2 = ik_ref[0]
        ik_even, ik_odd = ik2[:, :LANES], ik2[:, LANES:]
        sc_ref[...] = jnp.zeros((tq, tk), F32)

        for gi in range(N_IDX_PAIRS // IDX_PAIR_GROUP):
            lhs = iqs_ref[gi * grows:(gi + 1) * grows, :]
            raws = [lax.dot_general(lhs, kk, (((1,), (1,)), ((), ())), preferred_element_type=F32)
                    for kk in (ik_even, ik_odd)]
            part = None
            for j in range(IDX_PAIR_GROUP):
                for par in range(2):
                    wv = wb_ref[(gi * IDX_PAIR_GROUP + j) * 2 + par]
                    wv = jnp.concatenate([wv] * ncol, axis=1) if ncol > 1 else wv
                    term = wv * jnp.maximum(raws[par][j * tq:(j + 1) * tq, :], 0.0)
                    part = term if part is None else part + term
            sc_ref[...] += part

        sc = sc_ref[...] + 0.0
        bits = pltpu.bitcast(sc, I32)
        key = bits ^ ((bits >> 31) & INT_MAX)
        kpos = kb * tk + lax.broadcasted_iota(I32, (tq, tk), 1)
        qpos = q0 + lax.broadcasted_iota(I32, (tq, tk), 0)
        adm = jnp.logical_and((kpos >> CHUNK_SHIFT) <= (qpos >> CHUNK_SHIFT), kpos < n_keys)
        key = jnp.where(adm, key, INT_MIN)
        keys_ref[kb] = key
        k1 = key[:, :LANES]
        k2 = jnp.full((tq, LANES), INT_MIN, I32)
        for c in range(1, ncol):
            x = key[:, c * LANES:(c + 1) * LANES]
            k2 = jnp.maximum(k2, jnp.minimum(k1, x))
            k1 = jnp.maximum(k1, x)

        @pl.when(kb == 0)
        def _():
            kmax_ref[...] = k1
            k2nd_ref[...] = k2

        @pl.when(kb > 0)
        def _():
            o1 = kmax_ref[...]
            kmax_ref[...] = jnp.maximum(o1, k1)
            k2nd_ref[...] = jnp.maximum(jnp.minimum(o1, k1), jnp.maximum(k2nd_ref[...], k2))

    @pl.when(ph == 1)
    def _attend():
        def count_rows(pred):
            def blk(b, cnt):
                hit = pred(keys_ref[b], b).astype(I32)
                for c in range(ncol):
                    cnt = cnt + hit[:, c * LANES:(c + 1) * LANES]
                return cnt

            cnt = lax.fori_loop(0, nkb, blk, jnp.zeros((tq, LANES), I32))
            return jnp.sum(cnt, axis=1, keepdims=True)

        @pl.when(kb == 0)
        def _select():
            qrow = q0 + lax.broadcasted_iota(I32, (tq, 1), 0)
            n_adm = jnp.minimum(((qrow >> CHUNK_SHIFT) + 1) << CHUNK_SHIFT, n_keys)
            keep_all = n_adm <= topk
            lo0 = jnp.where(keep_all, INT_MIN + 1,
                            jnp.maximum(jnp.min(k2nd_ref[...], axis=1, keepdims=True), INT_MIN + 1))
            hi0 = jnp.max(kmax_ref[...], axis=1, keepdims=True) + 1
            cnt0 = count_rows(lambda x, b: x >= lo0)
            done0 = jnp.logical_or(keep_all, cnt0 == topk).astype(I32)

            def bis_cond(c):
                return jnp.logical_and(c[0] < MAX_BISECTIONS, c[5] > 0)

            def bis_body(c):
                it, lo, hi, cnt_lo, done, _ = c
                mid = (lo >> 1) + (hi >> 1) + (lo & hi & 1)
                tot = count_rows(lambda x, b: x >= mid)
                move = jnp.logical_and(tot >= topk, done == 0)
                shrink = jnp.logical_and(tot < topk, done == 0)
                cnt_new = jnp.where(move, tot, cnt_lo)
                done = jnp.where(jnp.logical_or(cnt_new == topk, mid == lo), 1, done)
                return (it + 1, jnp.where(move, mid, lo), jnp.where(shrink, mid, hi), cnt_new, done,
                        jnp.sum(1 - done))

            _, thr, _, cnt_thr, _, _ = lax.while_loop(
                bis_cond, bis_body, (jnp.int32(0), lo0, hi0, cnt0, done0, jnp.sum(1 - done0)))
            thr_ref[...] = thr
            pcut_ref[...] = jnp.full((tq, 1), INT_MAX, I32)

            tied = cnt_thr > topk

            @pl.when(jnp.max(tied.astype(I32)) > 0)
            def _ties():
                room = topk - count_rows(lambda x, b: x > thr)

                def pos_body(it, cut):
                    cand = cut + lax.shift_left(jnp.int32(1), pos_bits - 1 - it)

                    def pred(x, b):
                        kpos = b * tk + lax.broadcasted_iota(I32, (tq, tk), 1)
                        return jnp.logical_and(x == thr, kpos < cand)

                    return jnp.where(count_rows(pred) < room, cand, cut)

                cut = lax.fori_loop(0, pos_bits, pos_body, jnp.zeros((tq, 1), I32))
                pcut_ref[...] = jnp.where(tied, cut, INT_MAX)

            m_ref[...] = jnp.full(m_ref.shape, NEG_BIG, F32)
            acc_ref[...] = jnp.zeros(acc_ref.shape, F32)
            qv = q_ref[0]
            for n in range(N_KV_HEADS):
                for g in range(KV_GROUP):
                    hd = n * KV_GROUP + g
                    qs_ref[n, g * tq:(g + 1) * tq, :] = qv[:, hd * HEAD_DIM:(hd + 1) * HEAD_DIM]

        key = keys_ref[kb]
        thr = thr_ref[...]
        kpos = kb * tk + lax.broadcasted_iota(I32, (tq, tk), 1)
        keep = jnp.logical_or(key > thr, jnp.logical_and(key == thr, kpos <= pcut_ref[...]))
        bias = jnp.where(keep, 0.0, NEG_BIG).astype(F32)
        bias4 = jnp.concatenate([bias] * KV_GROUP, axis=0)
        kv = k_ref[0]
        vv = v_ref[0]
        ones = jnp.ones((tk, HEAD_DIM), BF16)
        ns = range(N_KV_HEADS)
        lgs = [lax.dot_general(qs_ref[n], kv[:, n * HEAD_DIM:(n + 1) * HEAD_DIM], (((1,), (1,)), ((), ())),
                               preferred_element_type=F32) + bias4 for n in ns]
        ps, alphas = [], []
        for n in ns:
            cols = [lgs[n][:, c * LANES:(c + 1) * LANES] for c in range(ncol)]
            mx = cols[0]
            for c in range(1, ncol):
                mx = jnp.maximum(mx, cols[c])
            m_prev = m_ref[n]
            m_new = jnp.maximum(m_prev, jnp.max(mx, axis=1, keepdims=True))
            alphas.append(jnp.exp2(m_prev - m_new))
            ps.append(jnp.concatenate([jnp.exp2(cols[c] - m_new).astype(BF16) for c in range(ncol)], axis=1))
            m_ref[n] = m_new
        for n in ns:
            ve = jnp.concatenate([vv[:, n * HEAD_DIM:(n + 1) * HEAD_DIM], ones], axis=1)
            pv = jnp.dot(ps[n], ve, preferred_element_type=F32)
            acc_ref[n] = jnp.concatenate([alphas[n], alphas[n]], axis=1) * acc_ref[n] + pv

        @pl.when(kb == nkb - 1)
        def _finish():
            for n in range(N_KV_HEADS):
                a = acc_ref[n]
                o = a[:, :HEAD_DIM] / a[:, HEAD_DIM:]
                for g in range(KV_GROUP):
                    hd = n * KV_GROUP + g
                    o_ref[0, :, hd * HEAD_DIM:(hd + 1) * HEAD_DIM] = o[g * tq:(g + 1) * tq, :].astype(BF16)


def _dsa_tables(batch, t, tq, tk, past, n_keys):
    rows = []
    for b in range(batch):
        for qb in range(t // tq):
            last_q = past + qb * tq + tq - 1
            limit = min(n_keys, (last_q // CHUNK + 1) * CHUNK)
            nkb = -(-limit // tk)
            for ph in range(2):
                for kb in range(nkb):
                    ikb = kb if ph == 0 else nkb - 1
                    kvb = 0 if ph == 0 else kb
                    rows.append((b, qb, ph, kb, nkb, ikb, kvb))
    tab = np.asarray(rows, dtype=np.int32).T
    return [jnp.asarray(tab[r]) for r in range(tab.shape[0])]


def dsa_attention(iq, iw, ik2, q, k, v, *, n_keys, past, tq, tk):
    batch, t, _ = iq.shape
    lp = ik2.shape[1]
    assert t % tq == 0 and lp % tk == 0 and tk % LANES == 0
    topk = min(IDX_TOPK_MAX, n_keys // 4)
    assert topk <= 2 * LANES
    tables = _dsa_tables(batch, t, tq, tk, past, n_keys)
    nsteps = int(tables[0].shape[0])
    nkb_max = lp // tk
    kern = functools.partial(_dsa_kernel, tq=tq, tk=tk, n_keys=n_keys, past=past, topk=topk,
                             pos_bits=max(1, (lp - 1).bit_length()))
    qmap = lambda s, tb, tqb, tph, tkb, tnkb, tikb, tkvb: (tb[s], tqb[s], 0)
    grid_spec = pltpu.PrefetchScalarGridSpec(
        num_scalar_prefetch=7,
        grid=(nsteps,),
        in_specs=[
            pl.BlockSpec((1, tq, IDX_WIDTH), qmap),
            pl.BlockSpec((1, tq, N_IDX_HEADS), qmap),
            pl.BlockSpec((1, tk, 2 * LANES), lambda s, tb, tqb, tph, tkb, tnkb, tikb, tkvb: (tb[s], tikb[s], 0)),
            pl.BlockSpec((1, tq, ATTN_WIDTH), qmap),
            pl.BlockSpec((1, tk, KV_WIDTH), lambda s, tb, tqb, tph, tkb, tnkb, tikb, tkvb: (tb[s], tkvb[s], 0)),
            pl.BlockSpec((1, tk, KV_WIDTH), lambda s, tb, tqb, tph, tkb, tnkb, tikb, tkvb: (tb[s], tkvb[s], 0)),
        ],
        out_specs=pl.BlockSpec((1, tq, ATTN_WIDTH), qmap),
        scratch_shapes=[
            pltpu.VMEM((nkb_max, tq, tk), I32),
            pltpu.VMEM((tq, tk), F32),
            pltpu.VMEM((tq, LANES), I32),
            pltpu.VMEM((tq, LANES), I32),
            pltpu.VMEM((tq, 1), I32),
            pltpu.VMEM((tq, 1), I32),
            pltpu.VMEM((N_IDX_HEADS, tq, LANES), F32),
            pltpu.VMEM((N_IDX_PAIRS * tq, LANES), BF16),
            pltpu.VMEM((N_KV_HEADS, KV_GROUP * tq, HEAD_DIM), BF16),
            pltpu.VMEM((N_KV_HEADS, KV_GROUP * tq, LANES), F32),
            pltpu.VMEM((N_KV_HEADS, KV_GROUP * tq, 2 * HEAD_DIM), F32),
        ],
    )
    return pl.pallas_call(
        kern,
        grid_spec=grid_spec,
        out_shape=jax.ShapeDtypeStruct((batch, t, ATTN_WIDTH), BF16),
        compiler_params=_cparams(("arbitrary",)),
        name="dsa_attention",
    )(*tables, iq, iw, ik2, q, k, v)


def _mm(a, b):
    return jnp.dot(a.astype(BF16), b.astype(BF16), preferred_element_type=F32)


def _mm_nt(a, b):
    return lax.dot_general(a.astype(BF16), b.astype(BF16), (((1,), (1,)), ((), ())), preferred_element_type=F32)


def _mm_tn(a, b):
    return lax.dot_general(a.astype(BF16), b.astype(BF16), (((0,), (0,)), ((), ())), preferred_element_type=F32)


def _delta_kernel(dq_ref, dk_ref, dv_ref, z_ref, bq_ref, bk_ref, bv_ref, cq_ref, ck_ref, cv_ref,
                  gc_ref, gr_ref, b_ref, gn_ref, s0_ref, o_ref, s_ref, eq_ref, ek_ref, ev_ref,
                  *, chunk, n_double):
    first = pl.program_id(1) == 0

    @pl.when(first)
    def _():
        s_ref[...] = s0_ref[...]
        for e_ref, buf_ref in ((eq_ref, bq_ref), (ek_ref, bk_ref), (ev_ref, bv_ref)):
            e_ref[0:SUBLANES, :] = buf_ref[0]

    def conv_silu(x_ref, cw_ref, e_ref):
        e_ref[SUBLANES:SUBLANES + chunk, :] = x_ref[0]
        w = cw_ref[...]
        y = e_ref[SUBLANES:SUBLANES + chunk, :] * w[DN_CONV - 1:DN_CONV, :]
        for i in range(DN_CONV - 1):
            off = SUBLANES - (DN_CONV - 1) + i
            y = y + e_ref[off:off + chunk, :] * w[i:i + 1, :]
        e_ref[0:SUBLANES, :] = e_ref[chunk:chunk + SUBLANES, :]
        return y * jax.nn.sigmoid(y)

    yq = conv_silu(dq_ref, cq_ref, eq_ref)
    yk = conv_silu(dk_ref, ck_ref, ek_ref)
    yv = conv_silu(dv_ref, cv_ref, ev_ref)

    row = lax.broadcasted_iota(I32, (chunk, chunk), 0)
    col = lax.broadcasted_iota(I32, (chunk, chunk), 1)
    eye = (row == col).astype(F32)
    gcs = gc_ref[0]
    grs = gr_ref[0, 0]
    betas = b_ref[0]
    gn = gn_ref[...]
    hs = range(DN_HEADS)
    sl = [slice(h * DN_HEAD_DIM, (h + 1) * DN_HEAD_DIM) for h in hs]

    def l2n(x):
        return x * lax.rsqrt(jnp.sum(x * x, axis=-1, keepdims=True) + EPS)

    kh = [l2n(yk[:, sl[h]]) for h in hs]
    qh = [l2n(yq[:, sl[h]]) * (DN_HEAD_DIM ** -0.5) for h in hs]
    gc = [gcs[:, h:h + 1] for h in hs]
    beta = [betas[:, h:h + 1] for h in hs]
    decay = [jnp.exp(jnp.where(row >= col, gc[h] - grs[h:h + 1, :], -jnp.inf)) for h in hs]
    kk = [_mm_nt(kh[h], kh[h]) for h in hs]
    qk = [_mm_nt(qh[h], kh[h]) * decay[h] for h in hs]
    x = [-jnp.where(row > col, beta[h] * kk[h] * decay[h], 0.0) for h in hs]
    parts = [x]
    for _ in range(n_double):
        x = [_mm(x[h], x[h]) for h in hs]
        parts.append(x)
    while len(parts) > 1:
        nxt = [[parts[i][h] + parts[i + 1][h] + _mm(parts[i][h], parts[i + 1][h]) for h in hs]
               for i in range(0, len(parts) - 1, 2)]
        if len(parts) % 2:
            nxt.append(parts[-1])
        parts = nxt
    t = [eye + parts[0][h] for h in hs]
    eg = [jnp.exp(gc[h]) for h in hs]
    uw = [_mm(t[h], jnp.concatenate([yv[:, sl[h]] * beta[h], kh[h] * (beta[h] * eg[h])], axis=1)) for h in hs]
    s = [s_ref[0, h] for h in hs]
    ws_qs = [_mm(jnp.concatenate([uw[h][:, DN_HEAD_DIM:], qh[h] * eg[h]], axis=0), s[h]) for h in hs]
    v_new = [uw[h][:, :DN_HEAD_DIM] - ws_qs[h][:chunk] for h in hs]
    gl = [gc[h][chunk - 1:chunk, :] for h in hs]
    o = [ws_qs[h][chunk:] + _mm(qk[h], v_new[h]) for h in hs]
    upd = [_mm_tn(kh[h] * jnp.exp(gl[h] - gc[h]), v_new[h]) for h in hs]
    for h in hs:
        s_ref[0, h] = s[h] * jnp.exp(gl[h]) + upd[h]
        ms = jnp.mean(o[h] * o[h], axis=-1, keepdims=True)
        on = o[h] * lax.rsqrt(ms + EPS) * gn
        z = z_ref[0, :, sl[h]]
        o_ref[0, :, sl[h]] = (on * (z * jax.nn.sigmoid(z))).astype(BF16)


def delta_rule(proj, dn_buf, conv_w, gc, beta, gnorm, state, chunk):
    batch, t, _ = proj.shape
    n = t // chunk
    n_double = max(0, (chunk - 1).bit_length() - 1)
    gr = jnp.swapaxes(gc.reshape(batch, n, chunk, DN_HEADS), 2, 3)
    buf8 = jnp.pad(dn_buf, ((0, 0), (SUBLANES - (DN_CONV - 1), 0), (0, 0)))
    kern = functools.partial(_delta_kernel, chunk=chunk, n_double=n_double)
    pcol = lambda c: pl.BlockSpec((1, chunk, DN_WIDTH), lambda b, i, c=c: (b, i, c // DN_WIDTH))
    bcol = lambda j: pl.BlockSpec((1, SUBLANES, DN_WIDTH), lambda b, i, j=j: (b, 0, j))
    wcol = lambda j: pl.BlockSpec((DN_CONV, DN_WIDTH), lambda b, i, j=j: (0, j))
    wide = pl.BlockSpec((1, chunk, DN_WIDTH), lambda b, i: (b, i, 0))
    narrow = pl.BlockSpec((1, chunk, DN_HEADS), lambda b, i: (b, i, 0))
    st = pl.BlockSpec((1, DN_HEADS, DN_HEAD_DIM, DN_HEAD_DIM), lambda b, i: (b, 0, 0, 0))
    return pl.pallas_call(
        kern,
        grid=(batch, n),
        in_specs=[pcol(COL_DQ), pcol(COL_DK), pcol(COL_DV), pcol(COL_DZ),
                  bcol(0), bcol(1), bcol(2), wcol(0), wcol(1), wcol(2),
                  narrow,
                  pl.BlockSpec((1, 1, DN_HEADS, chunk), lambda b, i: (b, i, 0, 0)),
                  narrow,
                  pl.BlockSpec((1, DN_HEAD_DIM), lambda b, i: (0, 0)),
                  st],
        out_specs=[wide, st],
        out_shape=[jax.ShapeDtypeStruct((batch, t, DN_WIDTH), BF16),
                   jax.ShapeDtypeStruct(state.shape, F32)],
        scratch_shapes=[pltpu.VMEM((chunk + SUBLANES, DN_WIDTH), F32)] * 3,
        compiler_params=_cparams(("parallel", "arbitrary")),
        name="delta_rule",
    )(proj, proj, proj, proj, buf8, buf8, buf8, conv_w, conv_w, conv_w,
      gc, gr, beta, gnorm.reshape(1, DN_HEAD_DIM), state)


def _layer(x, past_k, past_v, past_ik, dn_buf, dn_state, ffn_buf, lw, *, attn_tiles, layer):
    (norm1_g, w_in, dn_conv_w, dn_a_log, dn_dt_bias, dn_norm_g, w_out, norm2_g, ffn_w_up, ffn_conv_w,
     ffn_w_down) = lw
    b, t, d = x.shape
    past = past_k.shape[1]
    n_keys = past + t
    x2 = x.reshape(b * t, d)

    proj = norm_matmul(x2, norm1_g, w_in)
    pos = jnp.tile(past + jnp.arange(t, dtype=I32), b)
    q_bf, iq_bf, k_f32, k_bf, v_bf, ik2_bf, small = attn_prep(proj, pos)
    proj3 = proj.reshape(b, t, IN_COLS)
    small = small.reshape(b, t, LANES)
    new_k = k_f32.reshape(b, t, N_KV_HEADS, HEAD_DIM)
    new_v = proj3[:, :, COL_AV:COL_AV + KV_WIDTH].reshape(b, t, N_KV_HEADS, HEAD_DIM)
    new_ik = small[:, :, SMALL_IK:SMALL_IK + IDX_HEAD_DIM]
    i_w = small[:, :, SMALL_IW:SMALL_IW + N_IDX_HEADS]

    tq, tk = attn_tiles
    lp = -(-n_keys // tk) * tk

    def with_past(old_bf, new_bf):
        allk = jnp.concatenate([old_bf, new_bf.reshape(b, t, -1)], axis=1)
        return jnp.pad(allk, ((0, 0), (0, lp - n_keys), (0, 0)))

    pik = past_ik.astype(BF16)
    zpad = jnp.zeros_like(pik)
    past_ik2 = jnp.concatenate([pik, zpad, zpad, pik], axis=-1)
    attn = dsa_attention(
        iq_bf.reshape(b, t, IDX_WIDTH), i_w, with_past(past_ik2, ik2_bf), q_bf.reshape(b, t, ATTN_WIDTH),
        with_past(past_k.reshape(b, past, KV_WIDTH).astype(BF16), k_bf),
        with_past(past_v.reshape(b, past, KV_WIDTH).astype(BF16), v_bf),
        n_keys=n_keys, past=past, tq=tq, tk=tk)

    new_dn_buf = jnp.concatenate([dn_buf, proj3[:, :, COL_DQ:COL_DQ + 3 * DN_WIDTH]], axis=1)[:, t:] if t < DN_CONV - 1 \
        else proj3[:, t - (DN_CONV - 1):, COL_DQ:COL_DQ + 3 * DN_WIDTH]
    beta = jax.nn.sigmoid(small[:, :, SMALL_DB:SMALL_DB + DN_HEADS])
    g = -jnp.exp(dn_a_log) * jax.nn.softplus(small[:, :, SMALL_DA:SMALL_DA + DN_HEADS] + dn_dt_bias)
    chunk = CHUNK if t % CHUNK == 0 else t
    gc = jnp.cumsum(g.reshape(b, t // chunk, chunk, DN_HEADS), axis=2).reshape(b, t, DN_HEADS)
    o_dn, new_state = delta_rule(proj3, dn_buf, dn_conv_w, gc, beta, dn_norm_g, dn_state, chunk)

    x2 = matmul_res([attn.reshape(b * t, ATTN_WIDTH), o_dn.reshape(b * t, DN_WIDTH)], w_out, x2, layer)

    f2 = ffn_w_up.shape[2]
    xt = jnp.swapaxes(x2.reshape(b, t, d), 0, 1).reshape(t * b, d)
    halo = max(SUBLANES, (FFN_CONV - 1) * b)
    hist = jnp.swapaxes(ffn_buf, 0, 1).reshape((FFN_CONV - 1) * b, f2)
    hist = ffn_interleave(jnp.pad(hist, ((halo - hist.shape[0], 0), (0, 0))))
    act, tail = ffn_up(xt, norm2_g, ffn_w_up, ffn_conv_w, hist, step=b, layer=layer)
    tail = ffn_deinterleave(tail[halo - (FFN_CONV - 1) * b:])
    new_ffn_buf = jnp.swapaxes(tail.reshape(FFN_CONV - 1, b, f2), 0, 1)
    act = jnp.swapaxes(act.reshape(t, b, -1), 0, 1).reshape(b * t, -1)
    x2 = matmul_res([act], ffn_w_down, x2, layer)
    return x2.reshape(b, t, d), new_k, new_v, new_ik, new_dn_buf, new_state, new_ffn_buf


def kernel(x_prompt, x_sample, cache_k, cache_v, cache_kidx, state_dn_conv, state_dn, state_ffn_conv,
           norm1_g, w_in, dn_conv_w, dn_a_log, dn_dt_bias, dn_norm_g, w_out, norm2_g,
           ffn_w_up, ffn_conv_w, ffn_w_down, final_g):
    depth = w_in.shape[0]
    bp, tp, d = x_prompt.shape
    bs, ts, _ = x_sample.shape
    f2 = ffn_w_up.shape[2]
    zk = jnp.zeros((bp, 0, N_KV_HEADS, HEAD_DIM), F32)
    zik = jnp.zeros((bp, 0, IDX_HEAD_DIM), F32)
    z_dn_buf = jnp.zeros((bp, DN_CONV - 1, 3 * DN_WIDTH), F32)
    z_dn_state = jnp.zeros((bp, DN_HEADS, DN_HEAD_DIM, DN_HEAD_DIM), F32)
    z_ffn_buf = jnp.zeros((bp, FFN_CONV - 1, f2), F32)

    tq_p = _pick(tp, (256, 128, 64, 32, 16))
    tk_p = _pick(tp, (512, 256, 128))
    n_keys_s = cache_k.shape[2] + ts
    tk_s = -(-n_keys_s // LANES) * LANES

    w_out_bf, w_up_bf, w_down_bf = w_out.astype(BF16), ffn_w_up.astype(BF16), ffn_w_down.astype(BF16)

    xp, xs = x_prompt, x_sample
    st_p, st_s = [], []
    for l in range(depth):
        lw = (norm1_g[l], _regroup_in_proj(w_in, l), dn_conv_w[l], dn_a_log[l], dn_dt_bias[l],
              dn_norm_g[l], w_out_bf, norm2_g[l], w_up_bf, ffn_interleave(ffn_conv_w[l]), w_down_bf)
        xp, *sp = _layer(xp, zk, zk, zik, z_dn_buf, z_dn_state, z_ffn_buf, lw, attn_tiles=(tq_p, tk_p), layer=l)
        xs, *ss = _layer(xs, cache_k[l], cache_v[l], cache_kidx[l], state_dn_conv[l], state_dn[l],
                         state_ffn_conv[l], lw, attn_tiles=(ts, tk_s), layer=l)
        st_p.append(sp)
        st_s.append(ss)

    y_prompt = rms_norm_rows(xp.reshape(bp * tp, d), final_g).reshape(bp, tp, d)
    y_sample = rms_norm_rows(xs.reshape(bs * ts, d), final_g).reshape(bs, ts, d)

    def stacked(states, i):
        return jnp.stack([s[i] for s in states], axis=0)

    return (y_prompt, y_sample,
            *[stacked(st_p, i) for i in range(6)],
            *[stacked(st_s, i) for i in range(6)])
```

```python
import functools

import numpy as np
import jax
import jax.numpy as jnp
from jax import lax
from jax.experimental import pallas as pl
from jax.experimental.pallas import tpu as pltpu

F32 = jnp.float32
BF16 = jnp.bfloat16
I32 = jnp.int32

CHUNK = 64
CHUNK_SHIFT = CHUNK.bit_length() - 1
HEAD_DIM = 128
N_HEADS = 16
N_KV_HEADS = 4
KV_GROUP = N_HEADS // N_KV_HEADS
ATTN_WIDTH = N_HEADS * HEAD_DIM
KV_WIDTH = N_KV_HEADS * HEAD_DIM
N_IDX_HEADS = 32
IDX_HEAD_DIM = 64
IDX_WIDTH = N_IDX_HEADS * IDX_HEAD_DIM
IDX_SCALE = IDX_WIDTH ** -0.5
IDX_TOPK_MAX = 256
DN_HEADS = 16
DN_HEAD_DIM = 128
DN_WIDTH = DN_HEADS * DN_HEAD_DIM
DN_CONV = 4
FFN_CONV = 3
ROPE_THETA = 10000.0
EPS = 1e-6
LOG2E = 1.4426950408889634

LANES = 128
SUBLANES = 8
VMEM_LIMIT = 56 * 1024 * 1024

INT_MIN = -(2 ** 31)
INT_MAX = 2 ** 31 - 1
MAX_BISECTIONS = 32 + 2
NEG_BIG = -1e30

COL_AQ, COL_IQ, COL_DQ, COL_DK, COL_DV, COL_DZ = (i * ATTN_WIDTH for i in range(6))
COL_AK = 6 * ATTN_WIDTH
COL_AV = COL_AK + KV_WIDTH
COL_SMALL = COL_AV + KV_WIDTH
SMALL_IK, SMALL_IW, SMALL_DB, SMALL_DA = 0, 64, 96, 112
IN_TILE = 768
IN_COLS = -(-(COL_SMALL + LANES) // IN_TILE) * IN_TILE
assert ATTN_WIDTH == IDX_WIDTH == DN_WIDTH


def _cparams(sem):
    return pltpu.CompilerParams(dimension_semantics=sem, vmem_limit_bytes=VMEM_LIMIT)


def _regroup_kernel(w_ref, o_ref, *, segs):
    col = 0
    for a, b in segs:
        o_ref[:, col:col + (b - a)] = w_ref[:, a:b].astype(BF16)
        col += b - a
    o_ref[:, col:] = jnp.zeros((o_ref.shape[0], o_ref.shape[1] - col), BF16)


def _regroup_in_proj(w, layer):
    splits = (ATTN_WIDTH, KV_WIDTH, KV_WIDTH, IDX_WIDTH, IDX_HEAD_DIM, N_IDX_HEADS,
              DN_WIDTH, DN_WIDTH, DN_WIDTH, DN_HEADS, DN_HEADS, DN_WIDTH)
    starts = np.concatenate([[0], np.cumsum(splits)])
    seg = {n: (int(starts[i]), int(starts[i + 1])) for i, n in enumerate(
        ("a_q", "a_k", "a_v", "i_q", "i_k", "i_w", "d_q", "d_k", "d_v", "d_b", "d_a", "d_z"))}
    order = ("a_q", "i_q", "d_q", "d_k", "d_v", "d_z", "a_k", "a_v", "i_k", "i_w", "d_b", "d_a")
    _, d, cols = w.shape
    tr = _pick(d, (256, 128, 64, 32, 16, 8))
    return pl.pallas_call(
        functools.partial(_regroup_kernel, segs=tuple(seg[n] for n in order)),
        grid=(d // tr,),
        in_specs=[pl.BlockSpec((None, tr, cols), lambda i: (layer, i, 0))],
        out_specs=pl.BlockSpec((tr, IN_COLS), lambda i: (i, 0)),
        out_shape=jax.ShapeDtypeStruct((d, IN_COLS), BF16),
        compiler_params=_cparams(("parallel",)),
        name="regroup_in_proj",
    )(w)


def _pick(n, prefs):
    for p in prefs:
        if n % p == 0:
            return p
    raise ValueError(f"no tile in {prefs} divides {n}")


ROW_TILES = (512, 256, 128, 64, 32, 16, 8)


def _norm_matmul_kernel(x_ref, g_ref, w_ref, o_ref, h_ref):
    @pl.when(pl.program_id(1) == 0)
    def _():
        x = x_ref[...]
        ms = jnp.mean(x * x, axis=-1, keepdims=True)
        h_ref[...] = (x * lax.rsqrt(ms + EPS) * g_ref[...]).astype(BF16)

    o_ref[...] = jnp.dot(h_ref[...], w_ref[...], preferred_element_type=F32)


def norm_matmul(x, g, w):
    n, d = x.shape
    cols = w.shape[1]
    tm = _pick(n, ROW_TILES)
    tn = _pick(cols, (IN_TILE, 512, 256, 128))
    return pl.pallas_call(
        _norm_matmul_kernel,
        grid=(n // tm, cols // tn),
        in_specs=[pl.BlockSpec((tm, d), lambda i, j: (i, 0)),
                  pl.BlockSpec((1, d), lambda i, j: (0, 0)),
                  pl.BlockSpec((d, tn), lambda i, j: (0, j))],
        out_specs=pl.BlockSpec((tm, tn), lambda i, j: (i, j)),
        out_shape=jax.ShapeDtypeStruct((n, cols), F32),
        scratch_shapes=[pltpu.VMEM((tm, d), BF16)],
        compiler_params=_cparams(("parallel", "arbitrary")),
        name="norm_matmul",
    )(x, g.reshape(1, d), w)


def _matmul_res_kernel(*refs, n_lhs):
    a_refs, w_refs, x_ref, o_ref = refs[:n_lhs], refs[n_lhs:2 * n_lhs], refs[2 * n_lhs], refs[2 * n_lhs + 1]
    acc = x_ref[...]
    for a_ref, w_ref in zip(a_refs, w_refs):
        acc = acc + jnp.dot(a_ref[...], w_ref[...], preferred_element_type=F32)
    o_ref[...] = acc


def matmul_res(lhs, w, x, layer):
    n, k = lhs[0].shape
    cols = w.shape[2]
    tm = _pick(n, ROW_TILES)
    tn = _pick(cols, (256,) if k > 8192 else (1024, 512, 256, 128))
    kern = functools.partial(_matmul_res_kernel, n_lhs=len(lhs))
    return pl.pallas_call(
        kern,
        grid=(n // tm, cols // tn),
        in_specs=([pl.BlockSpec((tm, k), lambda i, j: (i, 0)) for _ in lhs]
                  + [pl.BlockSpec((None, k, tn), lambda i, j, r=r: (layer, r, j)) for r in range(len(lhs))]
                  + [pl.BlockSpec((tm, tn), lambda i, j: (i, j))]),
        out_specs=pl.BlockSpec((tm, tn), lambda i, j: (i, j)),
        out_shape=jax.ShapeDtypeStruct((n, cols), F32),
        compiler_params=_cparams(("parallel", "arbitrary")),
        name="matmul_res",
    )(*lhs, *([w] * len(lhs)), x)


def _rms_norm_kernel(x_ref, g_ref, o_ref):
    x = x_ref[...]
    ms = jnp.mean(x * x, axis=-1, keepdims=True)
    o_ref[...] = x * lax.rsqrt(ms + EPS) * g_ref[...]


def rms_norm_rows(x, g):
    n, d = x.shape
    tm = _pick(n, ROW_TILES)
    return pl.pallas_call(
        _rms_norm_kernel,
        grid=(n // tm,),
        in_specs=[pl.BlockSpec((tm, d), lambda i: (i, 0)), pl.BlockSpec((1, d), lambda i: (0, 0))],
        out_specs=pl.BlockSpec((tm, d), lambda i: (i, 0)),
        out_shape=jax.ShapeDtypeStruct((n, d), F32),
        compiler_params=_cparams(("parallel",)),
        name="rms_norm",
    )(x, g.reshape(1, d))


FFN_TILE = 256


def ffn_interleave(a):
    f = a.shape[-1] // 2
    lead = a.shape[:-1]
    return jnp.swapaxes(a.reshape(*lead, 2, f // FFN_TILE, FFN_TILE), -3, -2).reshape(*lead, 2 * f)


def ffn_deinterleave(a):
    f = a.shape[-1] // 2
    lead = a.shape[:-1]
    return jnp.swapaxes(a.reshape(*lead, f // FFN_TILE, 2, FFN_TILE), -3, -2).reshape(*lead, 2 * f)


def _ffn_up_kernel(x_ref, g_ref, wg_ref, wv_ref, c_ref, hist_ref, a_ref, st_ref, h_ref, e_ref, cy_ref,
                   *, tm, tn, step, halo):
    i = pl.program_id(0)
    j = pl.program_id(1)

    @pl.when(j == 0)
    def _():
        x = x_ref[...]
        ms = jnp.mean(x * x, axis=-1, keepdims=True)
        h_ref[...] = (x * lax.rsqrt(ms + EPS) * g_ref[...]).astype(BF16)

    h = h_ref[...]
    up = jnp.concatenate([jnp.dot(h, wg_ref[...], preferred_element_type=F32),
                          jnp.dot(h, wv_ref[...], preferred_element_type=F32)], axis=1)
    e_ref[halo:halo + tm, :] = up

    @pl.when(i == 0)
    def _():
        e_ref[0:halo, :] = hist_ref[...]

    @pl.when(i > 0)
    def _():
        e_ref[0:halo, :] = cy_ref[j]

    c = c_ref[...]
    y = (e_ref[halo - 2 * step:halo - 2 * step + tm, :] * c[0:1, :]
         + e_ref[halo - step:halo - step + tm, :] * c[1:2, :]
         + up * c[2:3, :])
    tail = e_ref[tm:tm + halo, :]
    cy_ref[j] = tail
    st_ref[...] = tail
    yg = y[:, :tn]
    a_ref[...] = (yg * jax.nn.sigmoid(yg) * y[:, tn:]).astype(BF16)


def ffn_up(x, g, w_up, conv_w, hist, step, layer):
    n, d = x.shape
    f2 = w_up.shape[2]
    f = f2 // 2
    halo = hist.shape[0]
    tm = _pick(n, ROW_TILES)
    tn = FFN_TILE
    nj = f // tn
    kern = functools.partial(_ffn_up_kernel, tm=tm, tn=tn, step=step, halo=halo)
    act, tail = pl.pallas_call(
        kern,
        grid=(n // tm, nj),
        in_specs=[pl.BlockSpec((tm, d), lambda i, j: (i, 0)),
                  pl.BlockSpec((1, d), lambda i, j: (0, 0)),
                  pl.BlockSpec((None, d, tn), lambda i, j: (layer, 0, j)),
                  pl.BlockSpec((None, d, tn), lambda i, j: (layer, 0, nj + j)),
                  pl.BlockSpec((FFN_CONV, 2 * tn), lambda i, j: (0, j)),
                  pl.BlockSpec((halo, 2 * tn), lambda i, j: (0, j))],
        out_specs=[pl.BlockSpec((tm, tn), lambda i, j: (i, j)),
                   pl.BlockSpec((None, halo, 2 * tn), lambda i, j: (i, 0, j))],
        out_shape=[jax.ShapeDtypeStruct((n, f), BF16),
                   jax.ShapeDtypeStruct((n // tm, halo, f2), F32)],
        scratch_shapes=[pltpu.VMEM((tm, d), BF16),
                        pltpu.VMEM((tm + halo, 2 * tn), F32),
                        pltpu.VMEM((nj, halo, 2 * tn), F32)],
        compiler_params=_cparams(("arbitrary", "arbitrary")),
        name="ffn_up",
    )(x, g.reshape(1, d), w_up, w_up, conv_w, hist)
    return act, tail[-1]


def _attn_prep_kernel(aq_ref, iq_ref, ak_ref, av_ref, sm_ref, cosa_ref, sina_ref, cosi_ref, sini_ref,
                      q_ref, iqo_ref, kf_ref, kb_ref, vb_ref, ik2_ref, smo_ref, *, tm):
    cosa, sina = cosa_ref[...], sina_ref[...]
    cosi, sini = cosi_ref[...], sini_ref[...]
    lane = lax.broadcasted_iota(I32, (tm, LANES), 1)
    first_half = (lane & (IDX_HEAD_DIM - 1)) < IDX_HEAD_DIM // 2

    def rot_head(x):
        return x * cosa + pltpu.roll(x, HEAD_DIM // 2, 1) * sina

    def rot_idx(x):
        partner = jnp.where(first_half, pltpu.roll(x, LANES - IDX_HEAD_DIM // 2, 1), pltpu.roll(x, IDX_HEAD_DIM // 2, 1))
        return x * cosi + partner * sini

    qscale = (HEAD_DIM ** -0.5) * LOG2E
    for h in range(N_HEADS):
        sl = slice(h * LANES, (h + 1) * LANES)
        q_ref[:, sl] = (rot_head(aq_ref[:, sl]) * qscale).astype(BF16)
        iqo_ref[:, sl] = rot_idx(iq_ref[:, sl]).astype(BF16)
    for n in range(N_KV_HEADS):
        sl = slice(n * LANES, (n + 1) * LANES)
        kr = rot_head(ak_ref[:, sl])
        kf_ref[:, sl] = kr
        kb_ref[:, sl] = kr.astype(BF16)
    vb_ref[...] = av_ref[...].astype(BF16)
    sm = sm_ref[...]
    ik = jnp.where(lane < SMALL_IW, rot_idx(sm), 0.0)
    smo_ref[...] = jnp.where(lane < SMALL_IW, ik, jnp.where(lane < SMALL_DB, sm * IDX_SCALE, sm))
    ik2_ref[:, :LANES] = ik.astype(BF16)
    ik2_ref[:, LANES:] = pltpu.roll(ik, IDX_HEAD_DIM, 1).astype(BF16)


def attn_prep(proj, pos):
    n = proj.shape[0]
    tm = _pick(n, (256, 128, 64, 32, 16, 8))
    posf = pos.astype(F32)[:, None]

    def tables(half):
        inv_freq = jnp.float32(ROPE_THETA) ** (-jnp.arange(half, dtype=F32) / half)
        ang = posf * inv_freq[None, :]
        cos, sin = jnp.cos(ang), jnp.sin(ang)
        rep = LANES // (2 * half)
        return jnp.tile(jnp.concatenate([cos, cos], axis=1), (1, rep)), jnp.tile(jnp.concatenate([-sin, sin], axis=1), (1, rep))

    cosa, sina = tables(HEAD_DIM // 2)
    cosi, sini = tables(IDX_HEAD_DIM // 2)
    wide = lambda c: pl.BlockSpec((tm, ATTN_WIDTH), lambda i, c=c: (i, c // ATTN_WIDTH))
    kvw = lambda c: pl.BlockSpec((tm, KV_WIDTH), lambda i, c=c: (i, c // KV_WIDTH))
    tab = pl.BlockSpec((tm, LANES), lambda i: (i, 0))
    row = lambda w: pl.BlockSpec((tm, w), lambda i: (i, 0))
    return pl.pallas_call(
        functools.partial(_attn_prep_kernel, tm=tm),
        grid=(n // tm,),
        in_specs=[wide(COL_AQ), wide(COL_IQ), kvw(COL_AK), kvw(COL_AV),
                  pl.BlockSpec((tm, LANES), lambda i: (i, COL_SMALL // LANES)), tab, tab, tab, tab],
        out_specs=[row(ATTN_WIDTH), row(IDX_WIDTH), row(KV_WIDTH), row(KV_WIDTH), row(KV_WIDTH), row(2 * LANES), row(LANES)],
        out_shape=[jax.ShapeDtypeStruct((n, ATTN_WIDTH), BF16),
                   jax.ShapeDtypeStruct((n, IDX_WIDTH), BF16),
                   jax.ShapeDtypeStruct((n, KV_WIDTH), F32),
                   jax.ShapeDtypeStruct((n, KV_WIDTH), BF16),
                   jax.ShapeDtypeStruct((n, KV_WIDTH), BF16),
                   jax.ShapeDtypeStruct((n, 2 * LANES), BF16),
                   jax.ShapeDtypeStruct((n, LANES), F32)],
        compiler_params=_cparams(("parallel",)),
        name="attn_prep",
    )(proj, proj, proj, proj, proj, cosa, sina, cosi, sini)


IDX_PAIR_GROUP = 8
N_IDX_PAIRS = N_IDX_HEADS // 2
COUNT_ROWS = 128


def _dsa_kernel(tb_ref, tqb_ref, tph_ref, tkb_ref, tnkb_ref, tikb_ref, tkvb_ref,
                iq_ref, iw_ref, ik_ref, q_ref, k_ref, v_ref, o_ref,
                keys_ref, sc_ref, kmax_ref, k2nd_ref, thr_ref, pcut_ref, wb_ref, iqs_ref, qs_ref, m_ref, acc_ref,
                *, tq, tk, n_keys, past, topk, pos_bits):
    s = pl.program_id(0)
    ph = tph_ref[s]
    kb = tkb_ref[s]
    nkb = tnkb_ref[s]
    q0 = tqb_ref[s] * tq + past
    ncol = tk // LANES
    grows = IDX_PAIR_GROUP * tq
    crows = min(COUNT_ROWS, tq)

    @pl.when(ph == 0)
    def _scores():
        @pl.when(kb == 0)
        def _():
            w = iw_ref[0]
            for h in range(N_IDX_HEADS):
                wb_ref[h] = jnp.broadcast_to(w[:, h:h + 1], (tq, LANES))
            for p in range(N_IDX_PAIRS):
                iqs_ref[p * tq:(p + 1) * tq, :] = iq_ref[0, :, p * LANES:(p + 1) * LANES]

        ik2 = ik_ref[0]
        ik_even, ik_odd = ik2[:, :LANES], ik2[:, LANES:]
        sc_ref[...] = jnp.zeros((tq, tk), F32)

        for gi in range(N_IDX_PAIRS // IDX_PAIR_GROUP):
            lhs = iqs_ref[gi * grows:(gi + 1) * grows, :]
            raws = [lax.dot_general(lhs, kk, (((1,), (1,)), ((), ())), preferred_element_type=F32)
                    for kk in (ik_even, ik_odd)]
            part = None
            for j in range(IDX_PAIR_GROUP):
                for par in range(2):
                    wv = wb_ref[(gi * IDX_PAIR_GROUP + j) * 2 + par]
                    wv = jnp.concatenate([wv] * ncol, axis=1) if ncol > 1 else wv
                    term = wv * jnp.maximum(raws[par][j * tq:(j + 1) * tq, :], 0.0)
                    part = term if part is None else part + term
            sc_ref[...] += part

        sc = sc_ref[...] + 0.0
        bits = pltpu.bitcast(sc, I32)
        key = bits ^ ((bits >> 31) & INT_MAX)
        kpos = kb * tk + lax.broadcasted_iota(I32, (tq, tk), 1)
        qpos = q0 + lax.broadcasted_iota(I32, (tq, tk), 0)
        adm = jnp.logical_and((kpos >> CHUNK_SHIFT) <= (qpos >> CHUNK_SHIFT), kpos < n_keys)
        key = jnp.where(adm, key, INT_MIN)
        keys_ref[kb] = key
        k1 = key[:, :LANES]
        k2 = jnp.full((tq, LANES), INT_MIN, I32)
        for c in range(1, ncol):
            x = key[:, c * LANES:(c + 1) * LANES]
            k2 = jnp.maximum(k2, jnp.minimum(k1, x))
            k1 = jnp.maximum(k1, x)

        @pl.when(kb == 0)
        def _():
            kmax_ref[...] = k1
            k2nd_ref[...] = k2

        @pl.when(kb > 0)
        def _():
            o1 = kmax_ref[...]
            kmax_ref[...] = jnp.maximum(o1, k1)
            k2nd_ref[...] = jnp.maximum(jnp.minimum(o1, k1), jnp.maximum(k2nd_ref[...], k2))

    @pl.when(ph == 1)
    def _attend():
        def count_rows(pred):
            outs = []
            for r0 in range(0, tq, crows):
                rs = slice(r0, r0 + crows)

                def blk(b, cnt, rs=rs):
                    hit = pred(keys_ref[b, rs, :], b, rs).astype(I32)
                    for c in range(ncol):
                        cnt = cnt + hit[:, c * LANES:(c + 1) * LANES]
                    return cnt

                cnt = lax.fori_loop(0, nkb, blk, jnp.zeros((crows, LANES), I32))
                outs.append(jnp.sum(cnt, axis=1, keepdims=True))
            return jnp.concatenate(outs, axis=0) if len(outs) > 1 else outs[0]

        @pl.when(kb == 0)
        def _select():
            qrow = q0 + lax.broadcasted_iota(I32, (tq, 1), 0)
            n_adm = jnp.minimum(((qrow >> CHUNK_SHIFT) + 1) << CHUNK_SHIFT, n_keys)
            keep_all = n_adm <= topk
            lo0 = jnp.where(keep_all, INT_MIN + 1,
                            jnp.maximum(jnp.min(k2nd_ref[...], axis=1, keepdims=True), INT_MIN + 1))
            hi0 = jnp.max(kmax_ref[...], axis=1, keepdims=True) + 1
            cnt0 = count_rows(lambda x, b, rs: x >= lo0[rs])
            done0 = jnp.logical_or(keep_all, cnt0 == topk).astype(I32)

            def bis_cond(c):
                return jnp.logical_and(c[0] < MAX_BISECTIONS, c[5] > 0)

            def bis_body(c):
                it, lo, hi, cnt_lo, done, _ = c
                mid = (lo >> 1) + (hi >> 1) + (lo & hi & 1)
                tot = count_rows(lambda x, b, rs: x >= mid[rs])
                move = jnp.logical_and(tot >= topk, done == 0)
                shrink = jnp.logical_and(tot < topk, done == 0)
                cnt_new = jnp.where(move, tot, cnt_lo)
                done = jnp.where(jnp.logical_or(cnt_new == topk, mid == lo), 1, done)
                return (it + 1, jnp.where(move, mid, lo), jnp.where(shrink, mid, hi), cnt_new, done,
                        jnp.sum(1 - done))

            _, thr, _, cnt_thr, _, _ = lax.while_loop(
                bis_cond, bis_body, (jnp.int32(0), lo0, hi0, cnt0, done0, jnp.sum(1 - done0)))
            thr_ref[...] = thr
            pcut_ref[...] = jnp.full((tq, 1), INT_MAX, I32)

            tied = cnt_thr > topk

            @pl.when(jnp.max(tied.astype(I32)) > 0)
            def _ties():
                room = topk - count_rows(lambda x, b, rs: x > thr[rs])

                def pos_body(it, cut):
                    cand = cut + lax.shift_left(jnp.int32(1), pos_bits - 1 - it)

                    def pred(x, b, rs):
                        kpos = b * tk + lax.broadcasted_iota(I32, (crows, tk), 1)
                        return jnp.logical_and(x == thr[rs], kpos < cand[rs])

                    return jnp.where(count_rows(pred) < room, cand, cut)

                cut = lax.fori_loop(0, pos_bits, pos_body, jnp.zeros((tq, 1), I32))
                pcut_ref[...] = jnp.where(tied, cut, INT_MAX)

            m_ref[...] = jnp.full(m_ref.shape, NEG_BIG, F32)
            acc_ref[...] = jnp.zeros(acc_ref.shape, F32)
            qv = q_ref[0]
            for n in range(N_KV_HEADS):
                for g in range(KV_GROUP):
                    hd = n * KV_GROUP + g
                    qs_ref[n, g * tq:(g + 1) * tq, :] = qv[:, hd * HEAD_DIM:(hd + 1) * HEAD_DIM]

        key = keys_ref[kb]
        thr = thr_ref[...]
        kpos = kb * tk + lax.broadcasted_iota(I32, (tq, tk), 1)
        keep = jnp.logical_or(key > thr, jnp.logical_and(key == thr, kpos <= pcut_ref[...]))
        bias = jnp.where(keep, 0.0, NEG_BIG).astype(F32)
        bias4 = jnp.concatenate([bias] * KV_GROUP, axis=0)
        kv = k_ref[0]
        vv = v_ref[0]
        ones = jnp.ones((tk, HEAD_DIM), BF16)
        ns = range(N_KV_HEADS)
        lgs = [lax.dot_general(qs_ref[n], kv[:, n * HEAD_DIM:(n + 1) * HEAD_DIM], (((1,), (1,)), ((), ())),
                               preferred_element_type=F32) + bias4 for n in ns]
        ps, alphas = [], []
        for n in ns:
            cols = [lgs[n][:, c * LANES:(c + 1) * LANES] for c in range(ncol)]
            mx = cols[0]
            for c in range(1, ncol):
                mx = jnp.maximum(mx, cols[c])
            m_prev = m_ref[n]
            m_new = jnp.maximum(m_prev, jnp.max(mx, axis=1, keepdims=True))
            alphas.append(jnp.exp2(m_prev - m_new))
            ps.append(jnp.concatenate([jnp.exp2(cols[c] - m_new).astype(BF16) for c in range(ncol)], axis=1))
            m_ref[n] = m_new
        for n in ns:
            ve = jnp.concatenate([vv[:, n * HEAD_DIM:(n + 1) * HEAD_DIM], ones], axis=1)
            pv = jnp.dot(ps[n], ve, preferred_element_type=F32)
            acc_ref[n] = jnp.concatenate([alphas[n], alphas[n]], axis=1) * acc_ref[n] + pv

        @pl.when(kb == nkb - 1)
        def _finish():
            for n in range(N_KV_HEADS):
                a = acc_ref[n]
                o = a[:, :HEAD_DIM] / a[:, HEAD_DIM:]
                for g in range(KV_GROUP):
                    hd = n * KV_GROUP + g
                    o_ref[0, :, hd * HEAD_DIM:(hd + 1) * HEAD_DIM] = o[g * tq:(g + 1) * tq, :].astype(BF16)


def _dsa_tables(batch, t, tq, tk, past, n_keys):
    rows = []
    for b in range(batch):
        for qb in range(t // tq):
            last_q = past + qb * tq + tq - 1
            limit = min(n_keys, (last_q // CHUNK + 1) * CHUNK)
            nkb = -(-limit // tk)
            for ph in range(2):
                for kb in range(nkb):
                    ikb = kb if ph == 0 else nkb - 1
                    kvb = 0 if ph == 0 else kb
                    rows.append((b, qb, ph, kb, nkb, ikb, kvb))
    tab = np.asarray(rows, dtype=np.int32).T
    return [jnp.asarray(tab[r]) for r in range(tab.shape[0])]


def dsa_attention(iq, iw, ik2, q, k, v, *, n_keys, past, tq, tk):
    batch, t, _ = iq.shape
    lp = ik2.shape[1]
    assert t % tq == 0 and lp % tk == 0 and tk % LANES == 0
    topk = min(IDX_TOPK_MAX, n_keys // 4)
    assert topk <= 2 * LANES
    tables = _dsa_tables(batch, t, tq, tk, past, n_keys)
    nsteps = int(tables[0].shape[0])
    nkb_max = lp // tk
    kern = functools.partial(_dsa_kernel, tq=tq, tk=tk, n_keys=n_keys, past=past, topk=topk,
                             pos_bits=max(1, (lp - 1).bit_length()))
    qmap = lambda s, tb, tqb, tph, tkb, tnkb, tikb, tkvb: (tb[s], tqb[s], 0)
    grid_spec = pltpu.PrefetchScalarGridSpec(
        num_scalar_prefetch=7,
        grid=(nsteps,),
        in_specs=[
            pl.BlockSpec((1, tq, IDX_WIDTH), qmap),
            pl.BlockSpec((1, tq, N_IDX_HEADS), qmap),
            pl.BlockSpec((1, tk, 2 * LANES), lambda s, tb, tqb, tph, tkb, tnkb, tikb, tkvb: (tb[s], tikb[s], 0)),
            pl.BlockSpec((1, tq, ATTN_WIDTH), qmap),
            pl.BlockSpec((1, tk, KV_WIDTH), lambda s, tb, tqb, tph, tkb, tnkb, tikb, tkvb: (tb[s], tkvb[s], 0)),
            pl.BlockSpec((1, tk, KV_WIDTH), lambda s, tb, tqb, tph, tkb, tnkb, tikb, tkvb: (tb[s], tkvb[s], 0)),
        ],
        out_specs=pl.BlockSpec((1, tq, ATTN_WIDTH), qmap),
        scratch_shapes=[
            pltpu.VMEM((nkb_max, tq, tk), I32),
            pltpu.VMEM((tq, tk), F32),
            pltpu.VMEM((tq, LANES), I32),
            pltpu.VMEM((tq, LANES), I32),
            pltpu.VMEM((tq, 1), I32),
            pltpu.VMEM((tq, 1), I32),
            pltpu.VMEM((N_IDX_HEADS, tq, LANES), F32),
            pltpu.VMEM((N_IDX_PAIRS * tq, LANES), BF16),
            pltpu.VMEM((N_KV_HEADS, KV_GROUP * tq, HEAD_DIM), BF16),
            pltpu.VMEM((N_KV_HEADS, KV_GROUP * tq, LANES), F32),
            pltpu.VMEM((N_KV_HEADS, KV_GROUP * tq, 2 * HEAD_DIM), F32),
        ],
    )
    return pl.pallas_call(
        kern,
        grid_spec=grid_spec,
        out_shape=jax.ShapeDtypeStruct((batch, t, ATTN_WIDTH), BF16),
        compiler_params=_cparams(("arbitrary",)),
        name="dsa_attention",
    )(*tables, iq, iw, ik2, q, k, v)


def _mm(a, b):
    return jnp.dot(a.astype(BF16), b.astype(BF16), preferred_element_type=F32)


def _mm_nt(a, b):
    return lax.dot_general(a.astype(BF16), b.astype(BF16), (((1,), (1,)), ((), ())), preferred_element_type=F32)


def _mm_tn(a, b):
    return lax.dot_general(a.astype(BF16), b.astype(BF16), (((0,), (0,)), ((), ())), preferred_element_type=F32)


def _delta_kernel(dq_ref, dk_ref, dv_ref, z_ref, bq_ref, bk_ref, bv_ref, cq_ref, ck_ref, cv_ref,
                  gc_ref, gr_ref, b_ref, gn_ref, s0_ref, o_ref, s_ref, eq_ref, ek_ref, ev_ref,
                  *, chunk, n_double):
    first = pl.program_id(1) == 0

    @pl.when(first)
    def _():
        s_ref[...] = s0_ref[...]
        for e_ref, buf_ref in ((eq_ref, bq_ref), (ek_ref, bk_ref), (ev_ref, bv_ref)):
            e_ref[0:SUBLANES, :] = buf_ref[0]

    def conv_silu(x_ref, cw_ref, e_ref):
        e_ref[SUBLANES:SUBLANES + chunk, :] = x_ref[0]
        w = cw_ref[...]
        y = e_ref[SUBLANES:SUBLANES + chunk, :] * w[DN_CONV - 1:DN_CONV, :]
        for i in range(DN_CONV - 1):
            off = SUBLANES - (DN_CONV - 1) + i
            y = y + e_ref[off:off + chunk, :] * w[i:i + 1, :]
        e_ref[0:SUBLANES, :] = e_ref[chunk:chunk + SUBLANES, :]
        return y * jax.nn.sigmoid(y)

    yq = conv_silu(dq_ref, cq_ref, eq_ref)
    yk = conv_silu(dk_ref, ck_ref, ek_ref)
    yv = conv_silu(dv_ref, cv_ref, ev_ref)

    row = lax.broadcasted_iota(I32, (chunk, chunk), 0)
    col = lax.broadcasted_iota(I32, (chunk, chunk), 1)
    eye = (row == col).astype(F32)
    gcs = gc_ref[0]
    grs = gr_ref[0, 0]
    betas = b_ref[0]
    gn = gn_ref[...]
    hs = range(DN_HEADS)
    sl = [slice(h * DN_HEAD_DIM, (h + 1) * DN_HEAD_DIM) for h in hs]

    def l2n(x):
        return x * lax.rsqrt(jnp.sum(x * x, axis=-1, keepdims=True) + EPS)

    kh = [l2n(yk[:, sl[h]]) for h in hs]
    qh = [l2n(yq[:, sl[h]]) * (DN_HEAD_DIM ** -0.5) for h in hs]
    gc = [gcs[:, h:h + 1] for h in hs]
    beta = [betas[:, h:h + 1] for h in hs]
    decay = [jnp.exp(jnp.where(row >= col, gc[h] - grs[h:h + 1, :], -jnp.inf)) for h in hs]
    kk = [_mm_nt(kh[h], kh[h]) for h in hs]
    qk = [_mm_nt(qh[h], kh[h]) * decay[h] for h in hs]
    x = [-jnp.where(row > col, beta[h] * kk[h] * decay[h], 0.0) for h in hs]
    parts = [x]
    for _ in range(n_double):
        x = [_mm(x[h], x[h]) for h in hs]
        parts.append(x)
    while len(parts) > 1:
        nxt = [[parts[i][h] + parts[i + 1][h] + _mm(parts[i][h], parts[i + 1][h]) for h in hs]
               for i in range(0, len(parts) - 1, 2)]
        if len(parts) % 2:
            nxt.append(parts[-1])
        parts = nxt
    t = [eye + parts[0][h] for h in hs]
    eg = [jnp.exp(gc[h]) for h in hs]
    uw = [_mm(t[h], jnp.concatenate([yv[:, sl[h]] * beta[h], kh[h] * (beta[h] * eg[h])], axis=1)) for h in hs]
    s = [s_ref[0, h] for h in hs]
    ws_qs = [_mm(jnp.concatenate([uw[h][:, DN_HEAD_DIM:], qh[h] * eg[h]], axis=0), s[h]) for h in hs]
    v_new = [uw[h][:, :DN_HEAD_DIM] - ws_qs[h][:chunk] for h in hs]
    gl = [gc[h][chunk - 1:chunk, :] for h in hs]
    o = [ws_qs[h][chunk:] + _mm(qk[h], v_new[h]) for h in hs]
    upd = [_mm_tn(kh[h] * jnp.exp(gl[h] - gc[h]), v_new[h]) for h in hs]
    for h in hs:
        s_ref[0, h] = s[h] * jnp.exp(gl[h]) + upd[h]
        ms = jnp.mean(o[h] * o[h], axis=-1, keepdims=True)
        on = o[h] * lax.rsqrt(ms + EPS) * gn
        z = z_ref[0, :, sl[h]]
        o_ref[0, :, sl[h]] = (on * (z * jax.nn.sigmoid(z))).astype(BF16)


def delta_rule(proj, dn_buf, conv_w, gc, beta, gnorm, state, chunk):
    batch, t, _ = proj.shape
    n = t // chunk
    n_double = max(0, (chunk - 1).bit_length() - 1)
    gr = jnp.swapaxes(gc.reshape(batch, n, chunk, DN_HEADS), 2, 3)
    buf8 = jnp.pad(dn_buf, ((0, 0), (SUBLANES - (DN_CONV - 1), 0), (0, 0)))
    kern = functools.partial(_delta_kernel, chunk=chunk, n_double=n_double)
    pcol = lambda c: pl.BlockSpec((1, chunk, DN_WIDTH), lambda b, i, c=c: (b, i, c // DN_WIDTH))
    bcol = lambda j: pl.BlockSpec((1, SUBLANES, DN_WIDTH), lambda b, i, j=j: (b, 0, j))
    wcol = lambda j: pl.BlockSpec((DN_CONV, DN_WIDTH), lambda b, i, j=j: (0, j))
    wide = pl.BlockSpec((1, chunk, DN_WIDTH), lambda b, i: (b, i, 0))
    narrow = pl.BlockSpec((1, chunk, DN_HEADS), lambda b, i: (b, i, 0))
    st = pl.BlockSpec((1, DN_HEADS, DN_HEAD_DIM, DN_HEAD_DIM), lambda b, i: (b, 0, 0, 0))
    return pl.pallas_call(
        kern,
        grid=(batch, n),
        in_specs=[pcol(COL_DQ), pcol(COL_DK), pcol(COL_DV), pcol(COL_DZ),
                  bcol(0), bcol(1), bcol(2), wcol(0), wcol(1), wcol(2),
                  narrow,
                  pl.BlockSpec((1, 1, DN_HEADS, chunk), lambda b, i: (b, i, 0, 0)),
                  narrow,
                  pl.BlockSpec((1, DN_HEAD_DIM), lambda b, i: (0, 0)),
                  st],
        out_specs=[wide, st],
        out_shape=[jax.ShapeDtypeStruct((batch, t, DN_WIDTH), BF16),
                   jax.ShapeDtypeStruct(state.shape, F32)],
        scratch_shapes=[pltpu.VMEM((chunk + SUBLANES, DN_WIDTH), F32)] * 3,
        compiler_params=_cparams(("parallel", "arbitrary")),
        name="delta_rule",
    )(proj, proj, proj, proj, buf8, buf8, buf8, conv_w, conv_w, conv_w,
      gc, gr, beta, gnorm.reshape(1, DN_HEAD_DIM), state)


def _layer(x, past_k, past_v, past_ik, dn_buf, dn_state, ffn_buf, lw, *, attn_tiles, layer):
    (norm1_g, w_in, dn_conv_w, dn_a_log, dn_dt_bias, dn_norm_g, w_out, norm2_g, ffn_w_up, ffn_conv_w,
     ffn_w_down) = lw
    b, t, d = x.shape
    past = past_k.shape[1]
    n_keys = past + t
    x2 = x.reshape(b * t, d)

    proj = norm_matmul(x2, norm1_g, w_in)
    pos = jnp.tile(past + jnp.arange(t, dtype=I32), b)
    q_bf, iq_bf, k_f32, k_bf, v_bf, ik2_bf, small = attn_prep(proj, pos)
    proj3 = proj.reshape(b, t, IN_COLS)
    small = small.reshape(b, t, LANES)
    new_k = k_f32.reshape(b, t, N_KV_HEADS, HEAD_DIM)
    new_v = proj3[:, :, COL_AV:COL_AV + KV_WIDTH].reshape(b, t, N_KV_HEADS, HEAD_DIM)
    new_ik = small[:, :, SMALL_IK:SMALL_IK + IDX_HEAD_DIM]
    i_w = small[:, :, SMALL_IW:SMALL_IW + N_IDX_HEADS]

    tq, tk = attn_tiles
    lp = -(-n_keys // tk) * tk

    def with_past(old_bf, new_bf):
        allk = jnp.concatenate([old_bf, new_bf.reshape(b, t, -1)], axis=1)
        return jnp.pad(allk, ((0, 0), (0, lp - n_keys), (0, 0)))

    pik = past_ik.astype(BF16)
    zpad = jnp.zeros_like(pik)
    past_ik2 = jnp.concatenate([pik, zpad, zpad, pik], axis=-1)
    attn = dsa_attention(
        iq_bf.reshape(b, t, IDX_WIDTH), i_w, with_past(past_ik2, ik2_bf), q_bf.reshape(b, t, ATTN_WIDTH),
        with_past(past_k.reshape(b, past, KV_WIDTH).astype(BF16), k_bf),
        with_past(past_v.reshape(b, past, KV_WIDTH).astype(BF16), v_bf),
        n_keys=n_keys, past=past, tq=tq, tk=tk)

    new_dn_buf = jnp.concatenate([dn_buf, proj3[:, :, COL_DQ:COL_DQ + 3 * DN_WIDTH]], axis=1)[:, t:] if t < DN_CONV - 1 \
        else proj3[:, t - (DN_CONV - 1):, COL_DQ:COL_DQ + 3 * DN_WIDTH]
    beta = jax.nn.sigmoid(small[:, :, SMALL_DB:SMALL_DB + DN_HEADS])
    g = -jnp.exp(dn_a_log) * jax.nn.softplus(small[:, :, SMALL_DA:SMALL_DA + DN_HEADS] + dn_dt_bias)
    chunk = CHUNK if t % CHUNK == 0 else t
    gc = jnp.cumsum(g.reshape(b, t // chunk, chunk, DN_HEADS), axis=2).reshape(b, t, DN_HEADS)
    o_dn, new_state = delta_rule(proj3, dn_buf, dn_conv_w, gc, beta, dn_norm_g, dn_state, chunk)

    x2 = matmul_res([attn.reshape(b * t, ATTN_WIDTH), o_dn.reshape(b * t, DN_WIDTH)], w_out, x2, layer)

    f2 = ffn_w_up.shape[2]
    xt = jnp.swapaxes(x2.reshape(b, t, d), 0, 1).reshape(t * b, d)
    halo = max(SUBLANES, (FFN_CONV - 1) * b)
    hist = jnp.swapaxes(ffn_buf, 0, 1).reshape((FFN_CONV - 1) * b, f2)
    hist = ffn_interleave(jnp.pad(hist, ((halo - hist.shape[0], 0), (0, 0))))
    act, tail = ffn_up(xt, norm2_g, ffn_w_up, ffn_conv_w, hist, step=b, layer=layer)
    tail = ffn_deinterleave(tail[halo - (FFN_CONV - 1) * b:])
    new_ffn_buf = jnp.swapaxes(tail.reshape(FFN_CONV - 1, b, f2), 0, 1)
    act = jnp.swapaxes(act.reshape(t, b, -1), 0, 1).reshape(b * t, -1)
    x2 = matmul_res([act], ffn_w_down, x2, layer)
    return x2.reshape(b, t, d), new_k, new_v, new_ik, new_dn_buf, new_state, new_ffn_buf


def kernel(x_prompt, x_sample, cache_k, cache_v, cache_kidx, state_dn_conv, state_dn, state_ffn_conv,
           norm1_g, w_in, dn_conv_w, dn_a_log, dn_dt_bias, dn_norm_g, w_out, norm2_g,
           ffn_w_up, ffn_conv_w, ffn_w_down, final_g):
    depth = w_in.shape[0]
    bp, tp, d = x_prompt.shape
    bs, ts, _ = x_sample.shape
    f2 = ffn_w_up.shape[2]
    zk = jnp.zeros((bp, 0, N_KV_HEADS, HEAD_DIM), F32)
    zik = jnp.zeros((bp, 0, IDX_HEAD_DIM), F32)
    z_dn_buf = jnp.zeros((bp, DN_CONV - 1, 3 * DN_WIDTH), F32)
    z_dn_state = jnp.zeros((bp, DN_HEADS, DN_HEAD_DIM, DN_HEAD_DIM), F32)
    z_ffn_buf = jnp.zeros((bp, FFN_CONV - 1, f2), F32)

    tq_p = _pick(tp, (256, 128, 64, 32, 16))
    tk_p = _pick(tp, (512, 256, 128))
    n_keys_s = cache_k.shape[2] + ts
    tk_s = -(-n_keys_s // LANES) * LANES

    w_out_bf, w_up_bf, w_down_bf = w_out.astype(BF16), ffn_w_up.astype(BF16), ffn_w_down.astype(BF16)

    xp, xs = x_prompt, x_sample
    st_p, st_s = [], []
    for l in range(depth):
        lw = (norm1_g[l], _regroup_in_proj(w_in, l), dn_conv_w[l], dn_a_log[l], dn_dt_bias[l],
              dn_norm_g[l], w_out_bf, norm2_g[l], w_up_bf, ffn_interleave(ffn_conv_w[l]), w_down_bf)
        xp, *sp = _layer(xp, zk, zk, zik, z_dn_buf, z_dn_state, z_ffn_buf, lw, attn_tiles=(tq_p, tk_p), layer=l)
        xs, *ss = _layer(xs, cache_k[l], cache_v[l], cache_kidx[l], state_dn_conv[l], state_dn[l],
                         state_ffn_conv[l], lw, attn_tiles=(ts, tk_s), layer=l)
        st_p.append(sp)
        st_s.append(ss)

    y_prompt = rms_norm_rows(xp.reshape(bp * tp, d), final_g).reshape(bp, tp, d)
    y_sample = rms_norm_rows(xs.reshape(bs * ts, d), final_g).reshape(bs, ts, d)

    def stacked(states, i):
        return jnp.stack([s[i] for s in states], axis=0)

    return (y_prompt, y_sample,
            *[stacked(st_p, i) for i in range(6)],
            *[stacked(st_s, i) for i in range(6)])
```
